```python
import functools
import jax, jax.numpy as jnp
from jax import lax
import numpy as np

D_MODEL = 1024
BATCH = 32
SEQ = 2048
DEPTH = 4
DEC_BATCH = 8
DEC_SEQ = 32
PAST_LEN = 1024

CHUNK = 64
N_META = 16
H_GLA = 4
DK_GLA = D_MODEL // 2 // H_GLA
DV_GLA = D_MODEL // H_GLA
GATE_RANK = 16
GATE_TAU = 16.0
H_RET = 4
DK_RET = D_MODEL // 2 // H_RET
DV_RET = D_MODEL // H_RET
D_FF = 2816
ROPE_BASE = 10000.0
LN_EPS = 1e-5
GN_EPS = 1e-5
DN_ALPHA = (2.0 * DEPTH) ** 0.25
DN_BETA = (8.0 * DEPTH) ** -0.25
IN_SIZES = (H_GLA * DK_GLA, H_GLA * DK_GLA, H_GLA * DV_GLA, H_GLA * DV_GLA, GATE_RANK,
            H_RET * DK_RET, H_RET * DK_RET, H_RET * DV_RET, H_RET * DV_RET, 2 * D_MODEL)
N_IN = sum(IN_SIZES)

kernel_name = 'gla_retnet_macaron_deepnorm_meta_stream'

F32 = jnp.float32


def layer_norm(x, g, b):
    xf = x.astype(F32)
    mu = xf.mean(-1, keepdims=True)
    var = jnp.mean(jnp.square(xf - mu), -1, keepdims=True)
    return ((xf - mu) * lax.rsqrt(var + LN_EPS) * g.astype(F32) + b.astype(F32)).astype(x.dtype)


def head_norm(o, g):
    mu = o.mean(-1, keepdims=True)
    var = jnp.mean(jnp.square(o - mu), -1, keepdims=True)
    on = (o - mu) * lax.rsqrt(var + GN_EPS)
    return on.reshape(o.shape[0], o.shape[1], -1) * g.astype(F32)


def swiglu(x, w_up, w_down):
    a, b = jnp.split(x @ w_up, 2, axis=-1)
    return (jax.nn.silu(a) * b) @ w_down


def rotary(x, pos):
    half = x.shape[-1] // 2
    inv = ROPE_BASE ** (-jnp.arange(half, dtype=F32) / half)
    ang = pos.astype(F32)[:, None] * inv[None, :]
    cos = jnp.cos(ang)[None, :, None, :]
    sin = jnp.sin(ang)[None, :, None, :]
    x1, x2 = x[..., :half], x[..., half:]
    return jnp.concatenate([x1 * cos - x2 * sin, x2 * cos + x1 * sin], axis=-1)


def gla_block(s0, q, k, v, lg):
    L = q.shape[1]
    b = jnp.cumsum(lg, axis=1)
    o_inter = jnp.einsum('blhd,bhde->blhe', q * jnp.exp(b), s0)
    causal = (jnp.arange(L)[:, None] >= jnp.arange(L)[None, :])[None, :, :, None, None]
    dec = jnp.exp(jnp.where(causal, b[:, :, None] - b[:, None, :], -jnp.inf))
    att = jnp.einsum('bthd,bshd,btshd->bhts', q, k, dec)
    o = o_inter + jnp.einsum('bhts,bshe->bthe', att, v)
    b_last = b[:, -1]
    s_new = jnp.exp(b_last)[..., None] * s0 + jnp.einsum(
        'bshd,bshe->bhde', k * jnp.exp(b_last[:, None] - b), v)
    return o, s_new


def ret_block(s0, q, k, v, log_gamma):
    L = q.shape[1]
    idx = jnp.arange(L, dtype=F32)
    inter = jnp.exp((idx + 1.0)[:, None] * log_gamma[None, :])
    o_inter = jnp.einsum('blhd,bhde->blhe', q, s0) * inter[None, :, :, None]
    rel = idx[:, None] - idx[None, :]
    dmat = jnp.exp(jnp.where(rel[..., None] >= 0, rel[..., None] * log_gamma, -jnp.inf))
    att = jnp.einsum('bthd,bshd->bhts', q, k) * jnp.transpose(dmat, (2, 0, 1))[None]
    o = o_inter + jnp.einsum('bhts,bshe->bthe', att, v)
    tail = jnp.exp((L - 1.0 - idx)[:, None] * log_gamma[None, :])
    s_new = jnp.exp(L * log_gamma)[None, :, None, None] * s0 + jnp.einsum(
        'bshd,bshe->bhde', k * tail[None, :, :, None], v)
    return o, s_new


def run_mixer(step, s0, xs, prompt):
    if not prompt:
        return step(s0, *xs)
    o_meta, s = step(s0, *tuple(a[:, :N_META] for a in xs))
    B, T = xs[0].shape[:2]
    t_rest = T - N_META
    n = t_rest // CHUNK
    chunks = tuple(a[:, N_META:].reshape(B, n, CHUNK, *a.shape[2:]).swapaxes(0, 1) for a in xs)

    def body(carry, c):
        o, carry = step(carry, *c)
        return carry, o

    s, o = lax.scan(body, s, chunks)
    o = o.swapaxes(0, 1).reshape(B, t_rest, *o.shape[3:])
    return jnp.concatenate([o_meta, o], axis=1), s


def layer(h, pos, s_gla, s_ret, prompt, ln_g, ln_b, w1u, w1d, w_in, w_a2, b_a, b_m,
          g_gla, g_ret, w_oa, w_ob, w_o, w2u, w2d):
    B, T, _ = h.shape
    h = layer_norm(DN_ALPHA * h + 0.5 * swiglu(h, w1u, w1d), ln_g[0], ln_b[0])
    z = h @ w_in
    cuts = [int(c) for c in np.cumsum(IN_SIZES)[:-1]]
    qg, kg, vg, rg, ag, qr, kr, vr, gr, mg = jnp.split(z, cuts, axis=-1)
    q = qg.astype(F32).reshape(B, T, H_GLA, DK_GLA) * DK_GLA ** -0.5
    k = kg.astype(F32).reshape(B, T, H_GLA, DK_GLA)
    v = vg.astype(F32).reshape(B, T, H_GLA, DV_GLA)
    lg = (jax.nn.log_sigmoid((ag @ w_a2 + b_a).astype(F32)) / GATE_TAU).reshape(B, T, H_GLA, DK_GLA)
    o_a, s_gla = run_mixer(gla_block, s_gla.astype(F32), (q, k, v, lg), prompt)
    ya = (jax.nn.silu(rg) * head_norm(o_a, g_gla).astype(h.dtype)) @ w_oa
    log_gamma = jnp.log1p(-(2.0 ** (-5.0 - jnp.arange(H_RET, dtype=F32))))
    q = rotary(qr.astype(F32).reshape(B, T, H_RET, DK_RET), pos)
    k = rotary(kr.astype(F32).reshape(B, T, H_RET, DK_RET), pos) * DK_RET ** -0.5
    v = vr.astype(F32).reshape(B, T, H_RET, DV_RET)
    step_r = functools.partial(ret_block, log_gamma=log_gamma)
    o_b, s_ret = run_mixer(step_r, s_ret.astype(F32), (q, k, v), prompt)
    yb = (jax.nn.silu(gr) * head_norm(o_b, g_ret).astype(h.dtype)) @ w_ob
    ga, gb = jnp.split(jax.nn.sigmoid(mg + b_m), 2, axis=-1)
    h = layer_norm(DN_ALPHA * h + (ga * ya + gb * yb) @ w_o, ln_g[1], ln_b[1])
    h = layer_norm(DN_ALPHA * h + 0.5 * swiglu(h, w2u, w2d), ln_g[2], ln_b[2])
    return h, s_gla, s_ret


def setup_inputs(seed: int = 0) -> dict:
    key = jax.random.key(seed)
    ks = jax.random.split(key, 24)
    n = jax.random.normal
    d = D_MODEL
    return {
        'x_prompt': n(ks[0], (BATCH, SEQ, d), F32),
        'x_sample': n(ks[1], (DEC_BATCH, DEC_SEQ, d), F32),
        'state_gla': n(ks[2], (DEPTH, DEC_BATCH, H_GLA, DK_GLA, DV_GLA), F32),
        'state_ret': n(ks[3], (DEPTH, DEC_BATCH, H_RET, DK_RET, DV_RET), F32),
        'meta': n(ks[4], (N_META, d), F32),
        'ln_g': 1.0 + 0.02 * n(ks[5], (DEPTH, 3, d), F32),
        'ln_b': 0.02 * n(ks[6], (DEPTH, 3, d), F32),
        'w_ffn1_up': n(ks[7], (DEPTH, d, 2 * D_FF), F32) * d ** -0.5,
        'w_ffn1_down': n(ks[8], (DEPTH, D_FF, d), F32) * (D_FF ** -0.5 * DN_BETA),
        'w_in': n(ks[9], (DEPTH, d, N_IN), F32) * d ** -0.5,
        'w_alpha2': n(ks[10], (DEPTH, GATE_RANK, H_GLA * DK_GLA), F32) * GATE_RANK ** -0.5,
        'b_alpha': 0.1 * n(ks[11], (DEPTH, H_GLA * DK_GLA), F32),
        'b_merge': 0.1 * n(ks[12], (DEPTH, 2 * d), F32),
        'gn_gla': 1.0 + 0.02 * n(ks[13], (DEPTH, H_GLA * DV_GLA), F32),
        'gn_ret': 1.0 + 0.02 * n(ks[14], (DEPTH, H_RET * DV_RET), F32),
        'w_o_gla': n(ks[15], (DEPTH, H_GLA * DV_GLA, d), F32) * ((H_GLA * DV_GLA) ** -0.5 * DN_BETA),
        'w_o_ret': n(ks[16], (DEPTH, H_RET * DV_RET, d), F32) * ((H_RET * DV_RET) ** -0.5 * DN_BETA),
        'w_out': n(ks[17], (DEPTH, d, d), F32) * (d ** -0.5 * DN_BETA),
        'w_ffn2_up': n(ks[18], (DEPTH, d, 2 * D_FF), F32) * d ** -0.5,
        'w_ffn2_down': n(ks[19], (DEPTH, D_FF, d), F32) * (D_FF ** -0.5 * DN_BETA),
    }


def reference(x_prompt, x_sample, state_gla, state_ret, meta, ln_g, ln_b, w_ffn1_up, w_ffn1_down,
              w_in, w_alpha2, b_alpha, b_merge, gn_gla, gn_ret, w_o_gla, w_o_ret, w_out,
              w_ffn2_up, w_ffn2_down):
    b_p = x_prompt.shape[0]
    hp = jnp.concatenate(
        [jnp.broadcast_to(meta[None].astype(x_prompt.dtype), (b_p, N_META, D_MODEL)), x_prompt], axis=1)
    hs = x_sample
    pos_p = jnp.arange(N_META + x_prompt.shape[1], dtype=jnp.int32)
    pos_s = N_META + PAST_LEN + jnp.arange(x_sample.shape[1], dtype=jnp.int32)
    zg = jnp.zeros((b_p, H_GLA, DK_GLA, DV_GLA), F32)
    zr = jnp.zeros((b_p, H_RET, DK_RET, DV_RET), F32)
    gla_p, ret_p, gla_s, ret_s = [], [], [], []
    for l in range(DEPTH):
        lw = (ln_g[l], ln_b[l], w_ffn1_up[l], w_ffn1_down[l], w_in[l], w_alpha2[l], b_alpha[l],
              b_merge[l], gn_gla[l], gn_ret[l], w_o_gla[l], w_o_ret[l], w_out[l],
              w_ffn2_up[l], w_ffn2_down[l])
        hp, sg, sr = layer(hp, pos_p, zg, zr, True, *lw)
        gla_p.append(sg.astype(x_prompt.dtype))
        ret_p.append(sr.astype(x_prompt.dtype))
        hs, sg, sr = layer(hs, pos_s, state_gla[l], state_ret[l], False, *lw)
        gla_s.append(sg.astype(state_gla.dtype))
        ret_s.append(sr.astype(state_ret.dtype))
    y_prompt = hp[:, N_META:]
    y_sample = hs
    new_gla_prompt = jnp.stack(gla_p, axis=0)
    new_ret_prompt = jnp.stack(ret_p, axis=0)
    new_gla_sample = jnp.stack(gla_s, axis=0)
    new_ret_sample = jnp.stack(ret_s, axis=0)
    return (y_prompt, y_sample, new_gla_prompt, new_ret_prompt, new_gla_sample, new_ret_sample)
```

```python
import functools

import numpy as np
import jax
import jax.numpy as jnp
from jax import lax
from jax.experimental import pallas as pl
from jax.experimental.pallas import tpu as pltpu

F32 = jnp.float32
BF16 = jnp.bfloat16

N_HEADS = 4
GATE_RANK_PAD = 128
GATE_TAU = 16.0
PAST_LEN = 1024
ROPE_BASE = 10000.0
LN_EPS = 1e-5
GN_EPS = 1e-5

SUBLANES = 8
MAIN_BLOCK = 256
SMALL_BLOCK = 128
ROW_TILE = 512
VMEM_LIMIT = 56 * 1024 * 1024

_NT = (((1,), (1,)), ((), ()))
_TN = (((0,), (0,)), ((), ()))


def _dot(a, b):
    return jnp.dot(a, b, preferred_element_type=F32)


def _layer_norm(x, g, b):
    mu = jnp.mean(x, axis=-1, keepdims=True)
    xc = x - mu
    var = jnp.mean(xc * xc, axis=-1, keepdims=True)
    return xc * lax.rsqrt(var + LN_EPS) * g + b


def _silu(x):
    return x * jax.nn.sigmoid(x)


def _log_sigmoid(x):
    return jnp.minimum(x, 0.0) - jnp.log1p(jnp.exp(-jnp.abs(x)))


def _resident(shape):
    return pl.BlockSpec(shape, lambda *_: (0,) * len(shape), pipeline_mode=pl.Buffered(1))


def _row_tile(n_rows):
    t = min(ROW_TILE, n_rows)
    while n_rows % t or t % SUBLANES:
        t -= SUBLANES
    return t


def _ffn_ln_kernel(h_ref, wup_ref, wdn_ref, g_ref, b_ref, o_ref, *, d_ff, fc, alpha):
    h = h_ref[...]
    hb = h.astype(BF16)
    acc = None
    for j in range(d_ff // fc):
        a = _dot(hb, wup_ref[:, j * fc:(j + 1) * fc])
        b = _dot(hb, wup_ref[:, d_ff + j * fc:d_ff + (j + 1) * fc])
        y = _dot((_silu(a) * b).astype(BF16), wdn_ref[j * fc:(j + 1) * fc, :])
        acc = y if acc is None else acc + y
    o_ref[...] = _layer_norm(alpha * h + 0.5 * acc, g_ref[...], b_ref[...])


def _ffn_ln(h, w_up, w_down, g, b, alpha):
    n, d = h.shape
    d_ff = w_down.shape[0]
    fc = d_ff // 2 if (d_ff // 2) % 128 == 0 else d_ff
    tm = _row_tile(n)
    return pl.pallas_call(
        functools.partial(_ffn_ln_kernel, d_ff=d_ff, fc=fc, alpha=alpha),
        grid=(n // tm,),
        in_specs=[
            pl.BlockSpec((tm, d), lambda i: (i, 0)),
            _resident(w_up.shape),
            _resident(w_down.shape),
            _resident(g.shape),
            _resident(b.shape),
        ],
        out_specs=pl.BlockSpec((tm, d), lambda i: (i, 0)),
        out_shape=jax.ShapeDtypeStruct((n, d), F32),
        compiler_params=pltpu.CompilerParams(
            dimension_semantics=("parallel",), vmem_limit_bytes=VMEM_LIMIT),
        name="ffn_ln",
    )(h, w_up, w_down, g, b)


def _out_kernel(h_ref, oa_ref, ob_ref, wg_ref, bm_ref, woa_ref, wob_ref, wo_ref, g_ref, b_ref,
                o_ref, *, d, alpha):
    h = h_ref[...]
    hb = h.astype(BF16)
    rg = _dot(hb, wg_ref[:, 0:d])
    ya = _dot((_silu(rg) * oa_ref[...]).astype(BF16), woa_ref[...])
    gr = _dot(hb, wg_ref[:, d:2 * d])
    yb = _dot((_silu(gr) * ob_ref[...]).astype(BF16), wob_ref[...])
    ga = jax.nn.sigmoid(_dot(hb, wg_ref[:, 2 * d:3 * d]) + bm_ref[:, 0:d])
    gb = jax.nn.sigmoid(_dot(hb, wg_ref[:, 3 * d:4 * d]) + bm_ref[:, d:2 * d])
    y = _dot((ga * ya + gb * yb).astype(BF16), wo_ref[...])
    o_ref[...] = _layer_norm(alpha * h + y, g_ref[...], b_ref[...])


def _out_proj(h, oa, ob, w_g, b_m, w_oa, w_ob, w_o, g, b, alpha):
    n, d = h.shape
    tm = _row_tile(n)
    row = pl.BlockSpec((tm, d), lambda i: (i, 0))
    return pl.pallas_call(
        functools.partial(_out_kernel, d=d, alpha=alpha),
        grid=(n // tm,),
        in_specs=[row, row, row, _resident(w_g.shape), _resident(b_m.shape), _resident(w_oa.shape),
                  _resident(w_ob.shape), _resident(w_o.shape), _resident(g.shape), _resident(b.shape)],
        out_specs=row,
        out_shape=jax.ShapeDtypeStruct((n, d), F32),
        compiler_params=pltpu.CompilerParams(
            dimension_semantics=("parallel",), vmem_limit_bytes=VMEM_LIMIT),
        name="out_proj",
    )(h, oa, ob, w_g, b_m, w_oa, w_ob, w_o, g, b)


def _head_norm(o, gain):
    mu = jnp.mean(o, axis=-1, keepdims=True)
    oc = o - mu
    var = jnp.mean(oc * oc, axis=-1, keepdims=True)
    return oc * lax.rsqrt(var + GN_EPS) * gain


def _mix_kernel(*refs, blk, n_lv, dk, dv, masked):
    if masked:
        (h_ref, wmix_ref, wag_ref, wa2_ref, ba_ref, cos_ref, sin_ref, decq_ref, deck_ref, tri_ref,
         lvl_ref, gla0_ref, ret0_ref, gna_ref, gnr_ref, gpow_ref, valid_ref,
         oa_ref, ob_ref, sg_ref, sr_ref, sgt_scr, sr_scr, b_scr) = refs
    else:
        (h_ref, wmix_ref, wag_ref, wa2_ref, ba_ref, cos_ref, sin_ref, decq_ref, deck_ref, tri_ref,
         lvl_ref, gla0_ref, ret0_ref, gna_ref, gnr_ref, gpow_ref,
         oa_ref, ob_ref, sg_ref, sr_ref, sgt_scr, sr_scr, b_scr) = refs
        valid_ref = None
    hk = N_HEADS * dk
    hv = N_HEADS * dv
    j = pl.program_id(1)

    @pl.when(j == 0)
    def _():
        for hd in range(N_HEADS):
            sgt_scr[hd] = gla0_ref[hd].T
        sr_scr[...] = ret0_ref[...]

    hb = h_ref[...].astype(BF16)
    qg = _dot(hb, wmix_ref[:, 0:hk]) * (dk ** -0.5)
    kg = _dot(hb, wmix_ref[:, hk:2 * hk])
    vg = _dot(hb, wmix_ref[:, 2 * hk:2 * hk + hv]).astype(BF16)
    c0 = 2 * hk + hv
    qr = _dot(hb, wmix_ref[:, c0:c0 + hk])
    kr = _dot(hb, wmix_ref[:, c0 + hk:c0 + 2 * hk])
    vr = _dot(hb, wmix_ref[:, c0 + 2 * hk:c0 + 2 * hk + hv]).astype(BF16)

    ag = _dot(hb, wag_ref[...])
    lg = _log_sigmoid(_dot(ag.astype(BF16), wa2_ref[...]) + ba_ref[...]) / GATE_TAU
    if masked:
        valid = valid_ref[...]
        vmask = jnp.concatenate([valid] * N_HEADS, axis=1)
        lg = lg * vmask
        kg = kg * vmask
    lg_hi = lg.astype(BF16)
    rem = lg - lg_hi.astype(F32)
    lg_mid = rem.astype(BF16)
    lg_lo = (rem - lg_mid.astype(F32)).astype(BF16)
    tri = tri_ref[...]
    bcum = _dot(tri, lg_hi) + _dot(tri, lg_mid) + _dot(tri, lg_lo)
    b_scr[...] = bcum

    def rows(r, n):
        return jnp.broadcast_to(b_scr[pl.ds(r, 1), :], (n, hk))

    def in_group_rows(s):
        return jnp.concatenate([rows(SUBLANES * i + s, SUBLANES) for i in range(blk // SUBLANES)], axis=0)

    r8 = lax.broadcasted_iota(jnp.int32, (blk, hk), 0) & (SUBLANES - 1)
    anchors = []
    for lv in range(n_lv):
        if lv == 0:
            a = jnp.where(r8 < 2, in_group_rows(0),
                          jnp.where(r8 < 4, in_group_rows(2),
                                    jnp.where(r8 < 6, in_group_rows(4), in_group_rows(6))))
        elif lv == 1:
            a = jnp.where(r8 < 4, in_group_rows(1), in_group_rows(5))
        elif lv == 2:
            a = in_group_rows(3)
        else:
            grp = 2 << lv
            a = jnp.concatenate([rows(g * grp + (grp // 2) - 1, grp) for g in range(blk // grp)], axis=0)
        anchors.append(a)
    b_last = rows(blk - 1, blk)

    fac = [jnp.exp(-jnp.abs(bcum - a)) for a in anchors]
    q_in = qg * jnp.exp(bcum)
    k_out = kg * jnp.exp(-jnp.abs(b_last - bcum))
    s_dec = jnp.exp(b_scr[pl.ds(blk - 1, 1), :])

    lvl = lvl_ref[...]
    gna = gna_ref[...]
    for hd in range(N_HEADS):
        ks = slice(hd * dk, (hd + 1) * dk)
        vs = slice(hd * dv, (hd + 1) * dv)
        q, k, v = qg[:, ks], kg[:, ks], vg[:, vs]
        att = jnp.where(lvl == n_lv,
                        lax.dot_general(q.astype(BF16), k.astype(BF16), _NT, preferred_element_type=F32),
                        0.0)
        for lv in range(n_lv):
            f = fac[lv][:, ks]
            g = lax.dot_general((q * f).astype(BF16), (k * f).astype(BF16), _NT,
                                preferred_element_type=F32)
            att = jnp.where(lvl == lv, g, att)
        st = sgt_scr[hd]
        o = _dot(att.astype(BF16), v) + lax.dot_general(
            q_in[:, ks].astype(BF16), st.astype(BF16), _NT, preferred_element_type=F32)
        sgt_scr[hd] = st * s_dec[:, ks] + lax.dot_general(
            v, k_out[:, ks].astype(BF16), _TN, preferred_element_type=F32)
        oa_ref[:, vs] = _head_norm(o, gna[:, vs])

    cos = cos_ref[...]
    sin = sin_ref[...]
    decq = decq_ref[...]
    deck = deck_ref[...]
    gpow = gpow_ref[...]
    gnr = gnr_ref[...]
    causal = lvl >= 0
    for hd in range(N_HEADS):
        ks = slice(hd * dk, (hd + 1) * dk)
        vs = slice(hd * dv, (hd + 1) * dv)
        q = qr[:, ks]
        k = kr[:, ks]
        q = (q * cos + pltpu.roll(q, dk // 2, 1) * sin) * decq[:, ks]
        k = (k * cos + pltpu.roll(k, dk // 2, 1) * sin) * deck[:, ks]
        if masked:
            k = k * valid
        qb = q.astype(BF16)
        kb = k.astype(BF16)
        v = vr[:, vs]
        att = jnp.where(causal, lax.dot_general(qb, kb, _NT, preferred_element_type=F32), 0.0)
        s0 = sr_scr[hd]
        o = _dot(att.astype(BF16), v) + _dot(qb, s0.astype(BF16))
        sr_scr[hd] = (s0 + lax.dot_general(kb, v, _TN, preferred_element_type=F32)) * gpow[:, vs]
        ob_ref[:, vs] = _head_norm(o, gnr[:, vs])

    @pl.when(j == pl.num_programs(1) - 1)
    def _():
        for hd in range(N_HEADS):
            sg_ref[hd] = sgt_scr[hd].T
        sr_ref[...] = sr_scr[...]


def _level_matrix(blk):
    n_lv = int(np.log2(blk))
    r = np.arange(blk)[:, None]
    c = np.arange(blk)[None, :]
    x = np.maximum(r ^ c, 1)
    lvl = np.where(r > c, np.floor(np.log2(x)).astype(np.int32), np.where(r == c, n_lv, -1))
    return lvl.astype(np.int32), n_lv


def _mix(h, gla0, ret0, w, tabs, blk, valid=None):
    nb, t, d = h.shape
    _, _, dk, dv = gla0.shape
    hk, hv = N_HEADS * dk, N_HEADS * dv
    assert t % blk == 0 and blk & (blk - 1) == 0 and SUBLANES <= blk <= 256
    lvl, n_lv = _level_matrix(blk)
    tri = jnp.asarray(np.tril(np.ones((blk, blk), np.float32)), BF16)
    masked = valid is not None

    def per_seq(a):
        return (lambda b, j: (b, j, 0)) if a.shape[0] > 1 else (lambda b, j: (0, j, 0))

    def state_map(a):
        return (lambda b, j: (b, 0, 0, 0)) if a.shape[0] > 1 else (lambda b, j: (0, 0, 0, 0))

    in_specs = [
        pl.BlockSpec((None, blk, d), lambda b, j: (b, j, 0)),
        _resident(w["mix"].shape), _resident(w["ag"].shape), _resident(w["a2"].shape),
        _resident(w["ba"].shape),
        pl.BlockSpec((None, blk, dk), per_seq(tabs["cos"])),
        pl.BlockSpec((None, blk, dk), per_seq(tabs["sin"])),
        _resident(tabs["decq"].shape), _resident(tabs["deck"].shape),
        _resident(tri.shape), _resident(lvl.shape),
        pl.BlockSpec((None, N_HEADS, dk, dv), state_map(gla0)),
        pl.BlockSpec((None, N_HEADS, dk, dv), state_map(ret0)),
        _resident(w["gna"].shape), _resident(w["gnr"].shape),
        pl.BlockSpec((None, 1, hv), (lambda b, j: (b, 0, 0)) if masked else (lambda b, j: (0, 0, 0))),
    ]
    args = [h, w["mix"], w["ag"], w["a2"], w["ba"], tabs["cos"], tabs["sin"], tabs["decq"], tabs["deck"],
            tri, jnp.asarray(lvl), gla0, ret0, w["gna"], w["gnr"], tabs["gpow"]]
    if masked:
        in_specs.append(pl.BlockSpec((None, blk, dk), lambda b, j: (b, j, 0)))
        args.append(valid)
    seq_out = pl.BlockSpec((None, blk, hv), lambda b, j: (b, j, 0))
    state_out = pl.BlockSpec((None, N_HEADS, dk, dv), lambda b, j: (b, 0, 0, 0))
    return pl.pallas_call(
        functools.partial(_mix_kernel, blk=blk, n_lv=n_lv, dk=dk, dv=dv, masked=masked),
        grid=(nb, t // blk),
        in_specs=in_specs,
        out_specs=[seq_out, seq_out, state_out, state_out],
        out_shape=[jax.ShapeDtypeStruct((nb, t, hv), F32), jax.ShapeDtypeStruct((nb, t, hv), F32),
                   jax.ShapeDtypeStruct((nb, N_HEADS, dk, dv), F32),
                   jax.ShapeDtypeStruct((nb, N_HEADS, dk, dv), F32)],
        scratch_shapes=[pltpu.VMEM((N_HEADS, dv, dk), F32), pltpu.VMEM((N_HEADS, dk, dv), F32),
                        pltpu.VMEM((blk, hk), F32)],
        compiler_params=pltpu.CompilerParams(
            dimension_semantics=("parallel", "arbitrary"), vmem_limit_bytes=VMEM_LIMIT),
        name="mix",
    )(*args)


def _mix_tables(pos, blk, n_valid, dk, dv):
    half = dk // 2
    inv = ROPE_BASE ** (-jnp.arange(half, dtype=F32) / half)
    ang = pos.astype(F32)[..., None] * inv
    cos = jnp.cos(ang)
    sin = jnp.sin(ang)
    log_gamma = jnp.log1p(-(2.0 ** (-5.0 - jnp.arange(N_HEADS, dtype=F32))))
    step = jnp.arange(blk, dtype=F32)[:, None, None] + 1.0
    dec = jnp.broadcast_to(step * log_gamma[None, :, None], (blk, N_HEADS, dk)).reshape(blk, N_HEADS * dk)
    gpow = jnp.exp(n_valid.astype(F32)[:, None, None] * log_gamma[None, :, None])
    gpow = jnp.broadcast_to(gpow, (pos.shape[0], N_HEADS, dv)).reshape(pos.shape[0], 1, N_HEADS * dv)
    return {
        "cos": jnp.concatenate([cos, cos], axis=-1),
        "sin": jnp.concatenate([-sin, sin], axis=-1),
        "decq": jnp.exp(dec),
        "deck": jnp.exp(-dec) * dk ** -0.5,
        "gpow": gpow,
    }


def kernel(x_prompt, x_sample, state_gla, state_ret, meta, ln_g, ln_b, w_ffn1_up, w_ffn1_down, w_in,
           w_alpha2, b_alpha, b_merge, gn_gla, gn_ret, w_o_gla, w_o_ret, w_out, w_ffn2_up, w_ffn2_down):
    nb, seq, d = x_prompt.shape
    nbs, seq_s, _ = x_sample.shape
    depth = ln_g.shape[0]
    n_meta = meta.shape[0]
    _, _, _, dk, dv = state_gla.shape
    hk, hv = N_HEADS * dk, N_HEADS * dv
    rank = w_alpha2.shape[1]
    alpha = (2.0 * depth) ** 0.25
    blk = min(MAIN_BLOCK, seq)
    sblk = SMALL_BLOCK
    assert n_meta <= sblk and seq_s <= sblk

    hs = jnp.zeros((1 + nbs, sblk, d), F32)
    hs = hs.at[0, :n_meta].set(meta.astype(F32)).at[1:, :seq_s].set(x_sample)
    n_valid_s = jnp.asarray([n_meta] + [seq_s] * nbs, jnp.int32)
    valid_s = (jnp.arange(sblk)[None, :, None] < n_valid_s[:, None, None]).astype(F32)
    valid_s = jnp.broadcast_to(valid_s, (1 + nbs, sblk, dk))
    off = jnp.asarray([0] + [n_meta + PAST_LEN] * nbs, jnp.int32)
    tabs_s = _mix_tables(off[:, None] + jnp.arange(sblk)[None, :], sblk, n_valid_s, dk, dv)
    tabs_p = _mix_tables(n_meta + jnp.arange(seq)[None, :], blk, jnp.asarray([blk], jnp.int32), dk, dv)
    hp = x_prompt

    c_rg = 2 * hk + hv
    c_ag = c_rg + hv
    c_qr = c_ag + rank
    c_gr = c_qr + 2 * hk + hv
    c_mg = c_gr + hv
    gla_p, ret_p, gla_s, ret_s = [], [], [], []
    for l in range(depth):
        wi = w_in[l]
        wmix = {
            "mix": jnp.concatenate([wi[:, :c_rg], wi[:, c_qr:c_gr]], axis=1).astype(BF16),
            "ag": jnp.pad(wi[:, c_ag:c_qr], ((0, 0), (0, GATE_RANK_PAD - rank))).astype(BF16),
            "a2": jnp.pad(w_alpha2[l], ((0, GATE_RANK_PAD - rank), (0, 0))).astype(BF16),
            "ba": b_alpha[l][None, :],
            "gna": gn_gla[l][None, :],
            "gnr": gn_ret[l][None, :],
        }
        w_g = jnp.concatenate([wi[:, c_rg:c_ag], wi[:, c_gr:c_mg], wi[:, c_mg:]], axis=1).astype(BF16)
        bm = b_merge[l][None, :]
        w_oa, w_ob, w_o = w_o_gla[l].astype(BF16), w_o_ret[l].astype(BF16), w_out[l].astype(BF16)
        w1u, w1d = w_ffn1_up[l].astype(BF16), w_ffn1_down[l].astype(BF16)
        w2u, w2d = w_ffn2_up[l].astype(BF16), w_ffn2_down[l].astype(BF16)
        g = [ln_g[l, i][None, :] for i in range(3)]
        b = [ln_b[l, i][None, :] for i in range(3)]

        def layer(h3, gla0, ret0, tabs, bsz, valid):
            shape = h3.shape
            h = _ffn_ln(h3.reshape(-1, d), w1u, w1d, g[0], b[0], alpha)
            oa, ob, sg, sr = _mix(h.reshape(shape), gla0, ret0, wmix, tabs, bsz, valid)
            h = _out_proj(h, oa.reshape(-1, hv), ob.reshape(-1, hv), w_g, bm, w_oa, w_ob, w_o,
                          g[1], b[1], alpha)
            h = _ffn_ln(h, w2u, w2d, g[2], b[2], alpha)
            return h.reshape(shape), sg, sr

        zero = jnp.zeros((1, N_HEADS, dk, dv), F32)
        hs, sg_s, sr_s = layer(hs, jnp.concatenate([zero, state_gla[l].astype(F32)], axis=0),
                               jnp.concatenate([zero, state_ret[l].astype(F32)], axis=0),
                               tabs_s, sblk, valid_s)
        hp, sg_p, sr_p = layer(hp, sg_s[:1], sr_s[:1], tabs_p, blk, None)
        gla_p.append(sg_p.astype(x_prompt.dtype))
        ret_p.append(sr_p.astype(x_prompt.dtype))
        gla_s.append(sg_s[1:].astype(state_gla.dtype))
        ret_s.append(sr_s[1:].astype(state_ret.dtype))

    return (hp, hs[1:, :seq_s], jnp.stack(gla_p, axis=0), jnp.stack(ret_p, axis=0),
            jnp.stack(gla_s, axis=0), jnp.stack(ret_s, axis=0))
```

```python
import functools

import numpy as np
import jax
import jax.numpy as jnp
from jax import lax
from jax.experimental import pallas as pl
from jax.experimental.pallas import tpu as pltpu

F32 = jnp.float32
BF16 = jnp.bfloat16

N_HEADS = 4
GATE_RANK_PAD = 128
GATE_TAU = 16.0
PAST_LEN = 1024
ROPE_BASE = 10000.0
LN_EPS = 1e-5
GN_EPS = 1e-5
LOG2E = 1.4426950408889634

SUBLANES = 8
MXU_WIDTH = 256
MAIN_BLOCK = 256
DIAG_BLOCK = 128
SMALL_BLOCK = 32
ROW_TILE = 1024
ROW_SUB = 256
FFN_CHUNK = 6 * MXU_WIDTH
VMEM_LIMIT = 56 * 1024 * 1024

_NT = (((1,), (1,)), ((), ()))
_TN = (((0,), (0,)), ((), ()))


def _dot(a, b):
    return jnp.dot(a, b, preferred_element_type=F32)


def _dot_nt(a, b):
    return lax.dot_general(a, b, _NT, preferred_element_type=F32)


def _dot_tn(a, b):
    return lax.dot_general(a, b, _TN, preferred_element_type=F32)


def _layer_norm(x, g, b):
    mu = jnp.mean(x, axis=-1, keepdims=True)
    xc = x - mu
    var = jnp.mean(xc * xc, axis=-1, keepdims=True)
    return xc * lax.rsqrt(var + LN_EPS) * g + b


def _silu(x):
    return x * jax.nn.sigmoid(x)


def _log_sigmoid(x):
    return jnp.minimum(x, 0.0) - jnp.log1p(jnp.exp(-jnp.abs(x)))


def _resident(shape):
    return pl.BlockSpec(shape, lambda *_: (0,) * len(shape), pipeline_mode=pl.Buffered(1))


def _row_tile(n_rows, limit):
    t = min(limit, n_rows)
    while n_rows % t or t % SUBLANES:
        t -= SUBLANES
    return t


def _ffn_ln_kernel(h_ref, wup_ref, wdn_ref, g_ref, b_ref, o_ref, *, d_ff, cuts, sub, alpha):
    n_sub = h_ref.shape[0] // sub
    n_ch = len(cuts) - 1
    items = [(s, j) for s in range(n_sub) for j in range(n_ch)]
    hb = [None] * n_sub
    acc = [None] * n_sub

    def up(s, j):
        if hb[s] is None:
            hb[s] = h_ref[s * sub:(s + 1) * sub, :].astype(BF16)
        return (_dot(hb[s], wup_ref[:, cuts[j]:cuts[j + 1]]),
                _dot(hb[s], wup_ref[:, d_ff + cuts[j]:d_ff + cuts[j + 1]]))

    def finish(s):
        rows = slice(s * sub, (s + 1) * sub)
        o_ref[rows, :] = _layer_norm(alpha * h_ref[rows, :] + 0.5 * acc[s], g_ref[...], b_ref[...])

    nxt = up(*items[0])
    ready = None
    for idx, (s, j) in enumerate(items):
        a, b = nxt
        if idx + 1 < len(items):
            nxt = up(*items[idx + 1])
        if ready is not None:
            finish(ready)
            ready = None
        y = _dot((_silu(a) * b).astype(BF16), wdn_ref[cuts[j]:cuts[j + 1], :])
        acc[s] = y if acc[s] is None else acc[s] + y
        if j == n_ch - 1:
            ready = s
    finish(ready)


def _ffn_ln(h, w_up, w_down, g, b, alpha):
    n, d = h.shape
    d_ff = w_down.shape[0]
    cuts = tuple(range(0, d_ff, FFN_CHUNK)) + (d_ff,)
    tm = _row_tile(n, ROW_TILE)
    sub = ROW_SUB if tm % ROW_SUB == 0 else tm
    return pl.pallas_call(
        functools.partial(_ffn_ln_kernel, d_ff=d_ff, cuts=cuts, sub=sub, alpha=alpha),
        grid=(n // tm,),
        in_specs=[
            pl.BlockSpec((tm, d), lambda i: (i, 0)),
            _resident(w_up.shape),
            _resident(w_down.shape),
            _resident(g.shape),
            _resident(b.shape),
        ],
        out_specs=pl.BlockSpec((tm, d), lambda i: (i, 0)),
        out_shape=jax.ShapeDtypeStruct((n, d), F32),
        compiler_params=pltpu.CompilerParams(
            dimension_semantics=("parallel",), vmem_limit_bytes=VMEM_LIMIT),
        name="ffn_ln",
    )(h, w_up, w_down, g, b)


def _head_norm(o, gain):
    mu = jnp.mean(o, axis=-1, keepdims=True)
    oc = o - mu
    var = jnp.mean(oc * oc, axis=-1, keepdims=True)
    return oc * lax.rsqrt(var + GN_EPS) * gain


def _attn_kernel(*refs, blk, dblk, dk, dv, masked, alpha):
    n_in = 24 if masked else 23
    (h_ref, wmix_ref, wag_ref, wa2_ref, ba_ref, cos_ref, sin_ref, decq_ref, deck_ref, tri_ref,
     lvl_ref, gla0_ref, ret0_ref, gna_ref, gnr_ref, gpow_ref, wg_ref, bm_ref, woa_ref, wob_ref,
     wo_ref, lng_ref, lnb_ref) = refs[:23]
    valid_ref = refs[23] if masked else None
    o_ref, sg_ref, sr_ref, sgt_scr, sr_scr, b_scr, oa_scr, ob_scr = refs[n_in:]
    hk = N_HEADS * dk
    hv = N_HEADS * dv
    d = h_ref.shape[-1]
    n_diag = blk // dblk
    lv_diag = dblk.bit_length() - 1
    n_lv = blk.bit_length() - 1
    j = pl.program_id(1)

    @pl.when(j == 0)
    def _():
        for hd in range(N_HEADS):
            sgt_scr[hd] = gla0_ref[hd].T
        sr_scr[...] = ret0_ref[...]

    h = h_ref[...]
    hb = h.astype(BF16)
    c0 = 2 * hk + hv
    pw = d // 2

    ag = _dot(hb, wag_ref[...])
    qg = _dot(hb, wmix_ref[:, 0:hk]) * (dk ** -0.5)
    lg = _log_sigmoid(_dot(ag.astype(BF16), wa2_ref[...]) + ba_ref[...]) / GATE_TAU
    kg = _dot(hb, wmix_ref[:, hk:2 * hk])
    if masked:
        valid = valid_ref[...]
        vmask = jnp.concatenate([valid] * N_HEADS, axis=1)
        lg = lg * vmask
        kg = kg * vmask
    lg_hi = lg.astype(BF16)
    rem = lg - lg_hi.astype(F32)
    lg_mid = rem.astype(BF16)
    lg_lo = (rem - lg_mid.astype(F32)).astype(BF16)
    tri = tri_ref[...]
    b2 = (_dot(tri, lg_hi) + _dot(tri, lg_mid) + _dot(tri, lg_lo)) * LOG2E
    b_scr[...] = b2

    def in_group_rows(s):
        return jnp.concatenate(
            [jnp.broadcast_to(b_scr[pl.ds(SUBLANES * i + s, 1), :], (SUBLANES, hk))
             for i in range(blk // SUBLANES)], axis=0)

    r8 = lax.broadcasted_iota(jnp.int32, (blk, hk), 0) & (SUBLANES - 1)

    def level_factor(lv):
        if lv == 0:
            a = jnp.where(r8 < 2, in_group_rows(0),
                          jnp.where(r8 < 4, in_group_rows(2),
                                    jnp.where(r8 < 6, in_group_rows(4), in_group_rows(6))))
            diff = b2 - a
        elif lv == 1:
            diff = b2 - jnp.where(r8 < 4, in_group_rows(1), in_group_rows(5))
        elif lv == 2:
            diff = b2 - in_group_rows(3)
        else:
            grp = 2 << lv
            diff = jnp.concatenate(
                [b2[g * grp:(g + 1) * grp] - b_scr[pl.ds(g * grp + grp // 2 - 1, 1), :]
                 for g in range(blk // grp)], axis=0)
        return jnp.exp2(-jnp.abs(diff))

    proj = [lambda: _dot(hb, wmix_ref[:, 2 * hk:2 * hk + hv]).astype(BF16),
            lambda: _dot(hb, wmix_ref[:, c0:c0 + hk]),
            lambda: _dot(hb, wmix_ref[:, c0 + hk:c0 + 2 * hk]),
            lambda: _dot(hb, wmix_ref[:, c0 + 2 * hk:c0 + 2 * hk + hv]).astype(BF16)]
    fac = []
    done = []
    for lv in range(n_lv):
        if 2 * len(done) <= lv and len(done) < len(proj):
            done.append(proj[len(done)]())
        fac.append(level_factor(lv))
    while len(done) < len(proj):
        done.append(proj[len(done)]())
    vg, qr, kr, vr = done
    q_in = (qg * jnp.exp2(b2)).astype(BF16)
    k_out = (kg * jnp.exp2(-jnp.abs(b_scr[pl.ds(blk - 1, 1), :] - b2))).astype(BF16)
    s_dec = jnp.exp2(b_scr[pl.ds(blk - 1, 1), :])

    gates = []

    def gate_piece():
        p = len(gates)
        x = _dot(hb, wg_ref[:, p * pw:(p + 1) * pw])
        if p * pw < 2 * d:
            gates.append(_silu(x))
        else:
            gates.append(jax.nn.sigmoid(x + bm_ref[:, p * pw - 2 * d:(p + 1) * pw - 2 * d]))

    def tile(x, i, hd):
        return x[i * dblk:(i + 1) * dblk, hd * dk:(hd + 1) * dk]

    lvl = lvl_ref[...]
    qgb = qg.astype(BF16)
    kgb = kg.astype(BF16)
    on_diag = lvl == lv_diag
    att = [[jnp.where(on_diag, _dot_nt(tile(qgb, i, hd), tile(kgb, i, hd)), 0.0)
            for hd in range(N_HEADS)] for i in range(n_diag)]
    for lv in range(lv_diag):
        qf = (qg * fac[lv]).astype(BF16)
        kf = (kg * fac[lv]).astype(BF16)
        at_level = lvl == lv
        for i in range(n_diag):
            for hd in range(N_HEADS):
                att[i][hd] = jnp.where(at_level, _dot_nt(tile(qf, i, hd), tile(kf, i, hd)), att[i][hd])
        if len(gates) < 4 * d // pw:
            gate_piece()
    while len(gates) < 4 * d // pw:
        gate_piece()
    rg, gr, ga, gb = [jnp.concatenate(gates[i * (d // pw):(i + 1) * (d // pw)], axis=1) for i in range(4)]
    if n_diag == 2:
        qf = (qg[dblk:] * fac[lv_diag][dblk:]).astype(BF16)
        kf = (kg[:dblk] * fac[lv_diag][:dblk]).astype(BF16)
    gna = gna_ref[...]
    for hd in range(N_HEADS):
        ks = slice(hd * dk, (hd + 1) * dk)
        vs = slice(hd * dv, (hd + 1) * dv)
        v = vg[:, vs]
        if n_diag == 2:
            below = _dot_nt(qf[:, ks], kf[:, ks])
            o = jnp.concatenate([
                _dot(att[0][hd].astype(BF16), v[:dblk]),
                _dot(jnp.concatenate([below, att[1][hd]], axis=1).astype(BF16), v)], axis=0)
        else:
            o = _dot(att[0][hd].astype(BF16), v)
        st = sgt_scr[hd]
        o = o + _dot_nt(q_in[:, ks], st.astype(BF16))
        sgt_scr[hd] = st * s_dec[:, ks] + _dot_tn(v, k_out[:, ks])
        oa_scr[:, vs] = _head_norm(o, gna[:, vs])

    cos = cos_ref[...]
    sin = sin_ref[...]
    decq = decq_ref[...]
    deck = deck_ref[...]
    gpow = gpow_ref[...]
    gnr = gnr_ref[...]
    causal = (lax.broadcasted_iota(jnp.int32, (blk, blk), 0)
              >= lax.broadcasted_iota(jnp.int32, (blk, blk), 1))
    for hd in range(N_HEADS):
        ks = slice(hd * dk, (hd + 1) * dk)
        vs = slice(hd * dv, (hd + 1) * dv)
        q = qr[:, ks]
        k = kr[:, ks]
        q = (q * cos + pltpu.roll(q, dk // 2, 1) * sin) * decq[:, ks]
        k = (k * cos + pltpu.roll(k, dk // 2, 1) * sin) * deck[:, ks]
        if masked:
            k = k * valid
        qb = q.astype(BF16)
        kb = k.astype(BF16)
        v = vr[:, vs]
        sc = jnp.where(causal, _dot_nt(qb, kb), 0.0)
        s0 = sr_scr[hd]
        o = _dot(sc.astype(BF16), v) + _dot(qb, s0.astype(BF16))
        sr_scr[hd] = (s0 + _dot_tn(kb, v)) * gpow[:, vs]
        ob_scr[:, vs] = _head_norm(o, gnr[:, vs])

    ya = _dot((rg * oa_scr[...]).astype(BF16), woa_ref[...])
    yb = _dot((gr * ob_scr[...]).astype(BF16), wob_ref[...])
    y = _dot((ga * ya + gb * yb).astype(BF16), wo_ref[...])
    o_ref[...] = _layer_norm(alpha * h + y, lng_ref[...], lnb_ref[...])

    @pl.when(j == pl.num_programs(1) - 1)
    def _():
        for hd in range(N_HEADS):
            sg_ref[hd] = sgt_scr[hd].T
        sr_ref[...] = sr_scr[...]


def _level_matrix(n):
    r = np.arange(n)[:, None]
    c = np.arange(n)[None, :]
    x = np.maximum(r ^ c, 1)
    lvl = np.where(r > c, np.floor(np.log2(x)), np.where(r == c, n.bit_length() - 1, -1))
    return lvl.astype(np.int32)


def _attn(h, gla0, ret0, w, tabs, blk, alpha, valid=None):
    nb, t, d = h.shape
    _, _, dk, dv = gla0.shape
    hk, hv = N_HEADS * dk, N_HEADS * dv
    dblk = min(blk, DIAG_BLOCK)
    assert t % blk == 0 and blk & (blk - 1) == 0 and SUBLANES <= blk <= 256 and blk // dblk <= 2
    lvl = jnp.asarray(_level_matrix(dblk))
    tri = jnp.asarray(np.tril(np.ones((blk, blk), np.float32)), BF16)
    masked = valid is not None

    def per_seq(a):
        return (lambda b, j: (b, j, 0)) if a.shape[0] > 1 else (lambda b, j: (0, j, 0))

    def state_map(a):
        return (lambda b, j: (b, 0, 0, 0)) if a.shape[0] > 1 else (lambda b, j: (0, 0, 0, 0))

    seq = pl.BlockSpec((None, blk, d), lambda b, j: (b, j, 0))
    args = [h, w["mix"], w["ag"], w["a2"], w["ba"], tabs["cos"], tabs["sin"], tabs["decq"], tabs["deck"],
            tri, lvl, gla0, ret0, w["gna"], w["gnr"], tabs["gpow"], w["g"], w["bm"], w["oa"], w["ob"],
            w["o"], w["lng"], w["lnb"]]
    in_specs = [
        seq,
        _resident(w["mix"].shape), _resident(w["ag"].shape), _resident(w["a2"].shape),
        _resident(w["ba"].shape),
        pl.BlockSpec((None, blk, dk), per_seq(tabs["cos"])),
        pl.BlockSpec((None, blk, dk), per_seq(tabs["sin"])),
        _resident(tabs["decq"].shape), _resident(tabs["deck"].shape),
        _resident(tri.shape), _resident(lvl.shape),
        pl.BlockSpec((None, N_HEADS, dk, dv), state_map(gla0)),
        pl.BlockSpec((None, N_HEADS, dk, dv), state_map(ret0)),
        _resident(w["gna"].shape), _resident(w["gnr"].shape),
        pl.BlockSpec((None, 1, hv), (lambda b, j: (b, 0, 0)) if masked else (lambda b, j: (0, 0, 0))),
        _resident(w["g"].shape), _resident(w["bm"].shape), _resident(w["oa"].shape),
        _resident(w["ob"].shape), _resident(w["o"].shape), _resident(w["lng"].shape),
        _resident(w["lnb"].shape),
    ]
    if masked:
        in_specs.append(pl.BlockSpec((None, blk, dk), lambda b, j: (b, j, 0)))
        args.append(valid)
    state_out = pl.BlockSpec((None, N_HEADS, dk, dv), lambda b, j: (b, 0, 0, 0))
    return pl.pallas_call(
        functools.partial(_attn_kernel, blk=blk, dblk=dblk, dk=dk, dv=dv, masked=masked, alpha=alpha),
        grid=(nb, t // blk),
        in_specs=in_specs,
        out_specs=[seq, state_out, state_out],
        out_shape=[jax.ShapeDtypeStruct((nb, t, d), F32),
                   jax.ShapeDtypeStruct((nb, N_HEADS, dk, dv), F32),
                   jax.ShapeDtypeStruct((nb, N_HEADS, dk, dv), F32)],
        scratch_shapes=[pltpu.VMEM((N_HEADS, dv, dk), F32), pltpu.VMEM((N_HEADS, dk, dv), F32),
                        pltpu.VMEM((blk, hk), F32), pltpu.VMEM((blk, hv), F32),
                        pltpu.VMEM((blk, hv), F32)],
        compiler_params=pltpu.CompilerParams(
            dimension_semantics=("parallel", "arbitrary"), vmem_limit_bytes=VMEM_LIMIT),
        name="attn",
    )(*args)


def _mix_tables(pos, blk, n_valid, dk, dv):
    half = dk // 2
    inv = ROPE_BASE ** (-jnp.arange(half, dtype=F32) / half)
    ang = pos.astype(F32)[..., None] * inv
    cos = jnp.cos(ang)
    sin = jnp.sin(ang)
    log_gamma = jnp.log1p(-(2.0 ** (-5.0 - jnp.arange(N_HEADS, dtype=F32))))
    step = jnp.arange(blk, dtype=F32)[:, None, None] + 1.0
    dec = jnp.broadcast_to(step * log_gamma[None, :, None], (blk, N_HEADS, dk)).reshape(blk, N_HEADS * dk)
    gpow = jnp.exp(n_valid.astype(F32)[:, None, None] * log_gamma[None, :, None])
    gpow = jnp.broadcast_to(gpow, (pos.shape[0], N_HEADS, dv)).reshape(pos.shape[0], 1, N_HEADS * dv)
    return {
        "cos": jnp.concatenate([cos, cos], axis=-1),
        "sin": jnp.concatenate([-sin, sin], axis=-1),
        "decq": jnp.exp(dec),
        "deck": jnp.exp(-dec) * dk ** -0.5,
        "gpow": gpow,
    }


def kernel(x_prompt, x_sample, state_gla, state_ret, meta, ln_g, ln_b, w_ffn1_up, w_ffn1_down, w_in,
           w_alpha2, b_alpha, b_merge, gn_gla, gn_ret, w_o_gla, w_o_ret, w_out, w_ffn2_up, w_ffn2_down):
    nb, seq, d = x_prompt.shape
    nbs, seq_s, _ = x_sample.shape
    depth = ln_g.shape[0]
    n_meta = meta.shape[0]
    _, _, _, dk, dv = state_gla.shape
    hk, hv = N_HEADS * dk, N_HEADS * dv
    rank = w_alpha2.shape[1]
    alpha = (2.0 * depth) ** 0.25
    blk = min(MAIN_BLOCK, seq)
    sblk = SMALL_BLOCK
    assert n_meta <= sblk and seq_s <= sblk

    hs = jnp.zeros((1 + nbs, sblk, d), F32)
    hs = hs.at[0, :n_meta].set(meta.astype(F32)).at[1:, :seq_s].set(x_sample)
    n_valid_s = jnp.asarray([n_meta] + [seq_s] * nbs, jnp.int32)
    valid_s = (jnp.arange(sblk)[None, :, None] < n_valid_s[:, None, None]).astype(F32)
    valid_s = jnp.broadcast_to(valid_s, (1 + nbs, sblk, dk))
    off = jnp.asarray([0] + [n_meta + PAST_LEN] * nbs, jnp.int32)
    tabs_s = _mix_tables(off[:, None] + jnp.arange(sblk)[None, :], sblk, n_valid_s, dk, dv)
    tabs_p = _mix_tables(n_meta + jnp.arange(seq)[None, :], blk, jnp.asarray([blk], jnp.int32), dk, dv)
    hp = x_prompt

    c_rg = 2 * hk + hv
    c_ag = c_rg + hv
    c_qr = c_ag + rank
    c_gr = c_qr + 2 * hk + hv
    c_mg = c_gr + hv
    gla_p, ret_p, gla_s, ret_s = [], [], [], []
    for l in range(depth):
        wi = w_in[l]
        wa = {
            "mix": jnp.concatenate([wi[:, :c_rg], wi[:, c_qr:c_gr]], axis=1).astype(BF16),
            "ag": jnp.pad(wi[:, c_ag:c_qr], ((0, 0), (0, GATE_RANK_PAD - rank))).astype(BF16),
            "a2": jnp.pad(w_alpha2[l], ((0, GATE_RANK_PAD - rank), (0, 0))).astype(BF16),
            "ba": b_alpha[l][None, :],
            "gna": gn_gla[l][None, :],
            "gnr": gn_ret[l][None, :],
            "g": jnp.concatenate([wi[:, c_rg:c_ag], wi[:, c_gr:c_mg], wi[:, c_mg:]], axis=1).astype(BF16),
            "bm": b_merge[l][None, :],
            "oa": w_o_gla[l].astype(BF16),
            "ob": w_o_ret[l].astype(BF16),
            "o": w_out[l].astype(BF16),
            "lng": ln_g[l, 1][None, :],
            "lnb": ln_b[l, 1][None, :],
        }
        w1u, w1d = w_ffn1_up[l].astype(BF16), w_ffn1_down[l].astype(BF16)
        w2u, w2d = w_ffn2_up[l].astype(BF16), w_ffn2_down[l].astype(BF16)

        def layer(h3, gla0, ret0, tabs, bsz, valid):
            shape = h3.shape
            h = _ffn_ln(h3.reshape(-1, d), w1u, w1d, ln_g[l, 0][None, :], ln_b[l, 0][None, :], alpha)
            h, sg, sr = _attn(h.reshape(shape), gla0, ret0, wa, tabs, bsz, alpha, valid)
            h = _ffn_ln(h.reshape(-1, d), w2u, w2d, ln_g[l, 2][None, :], ln_b[l, 2][None, :], alpha)
            return h.reshape(shape), sg, sr

        zero = jnp.zeros((1, N_HEADS, dk, dv), F32)
        hs, sg_s, sr_s = layer(hs, jnp.concatenate([zero, state_gla[l].astype(F32)], axis=0),
                               jnp.concatenate([zero, state_ret[l].astype(F32)], axis=0),
                               tabs_s, sblk, valid_s)
        hp, sg_p, sr_p = layer(hp, sg_s[:1], sr_s[:1], tabs_p, blk, None)
        gla_p.append(sg_p.astype(x_prompt.dtype))
        ret_p.append(sr_p.astype(x_prompt.dtype))
        gla_s.append(sg_s[1:].astype(state_gla.dtype))
        ret_s.append(sr_s[1:].astype(state_ret.dtype))

    return (hp, hs[1:, :seq_s], jnp.stack(gla_p, axis=0), jnp.stack(ret_p, axis=0),
            jnp.stack(gla_s, axis=0), jnp.stack(ret_s, axis=0))
```

```python
import functools

import numpy as np
import jax
import jax.numpy as jnp
from jax import lax
from jax.experimental import pallas as pl
from jax.experimental.pallas import tpu as pltpu

F32 = jnp.float32
BF16 = jnp.bfloat16

N_HEADS = 4
GATE_RANK_PAD = 128
GATE_TAU = 16.0
PAST_LEN = 1024
ROPE_BASE = 10000.0
LN_EPS = 1e-5
GN_EPS = 1e-5
LOG2E = 1.4426950408889634

SUBLANES = 8
MXU_WIDTH = 256
MAIN_BLOCK = 256
MAIN_SEQS = 1
SEQ_STAGGER = 12
DIAG_BLOCK = 128
SMALL_BLOCK = 32
ROW_TILE = 1024
ROW_SUB = 256
FFN_CHUNK = 6 * MXU_WIDTH
VMEM_LIMIT = 56 * 1024 * 1024

_NT = (((1,), (1,)), ((), ()))
_TN = (((0,), (0,)), ((), ()))


def _dot(a, b):
    return jnp.dot(a, b, preferred_element_type=F32)


def _dot_nt(a, b):
    return lax.dot_general(a, b, _NT, preferred_element_type=F32)


def _dot_tn(a, b):
    return lax.dot_general(a, b, _TN, preferred_element_type=F32)


def _layer_norm(x, g, b):
    mu = jnp.mean(x, axis=-1, keepdims=True)
    xc = x - mu
    var = jnp.mean(xc * xc, axis=-1, keepdims=True)
    return xc * lax.rsqrt(var + LN_EPS) * g + b


def _sigmoid(x):
    return 0.5 + 0.5 * jnp.tanh(0.5 * x)


def _silu(x):
    hx = 0.5 * x
    return hx + hx * jnp.tanh(hx)


def _log_sigmoid(x):
    return jnp.minimum(x, 0.0) - jnp.log1p(jnp.exp(-jnp.abs(x)))


def _resident(shape):
    return pl.BlockSpec(shape, lambda *_: (0,) * len(shape), pipeline_mode=pl.Buffered(1))


def _layer_of(stacked, layer):
    rest = stacked.shape[1:]
    return pl.BlockSpec((None,) + rest, lambda *_: (layer,) + (0,) * len(rest),
                        pipeline_mode=pl.Buffered(1))


def _row_tile(n_rows, limit):
    t = min(limit, n_rows)
    while n_rows % t or t % SUBLANES:
        t -= SUBLANES
    return t


def _ffn_ln_kernel(h_ref, wup_ref, wdn_ref, g_ref, b_ref, o_ref, *, d_ff, cuts, sub, alpha):
    n_sub = h_ref.shape[0] // sub
    n_ch = len(cuts) - 1
    items = [(s, j) for s in range(n_sub) for j in range(n_ch)]
    hb = [None] * n_sub
    acc = [None] * n_sub

    def up(s, j):
        if hb[s] is None:
            hb[s] = h_ref[s * sub:(s + 1) * sub, :].astype(BF16)
        return (_dot(hb[s], wup_ref[:, cuts[j]:cuts[j + 1]]),
                _dot(hb[s], wup_ref[:, d_ff + cuts[j]:d_ff + cuts[j + 1]]))

    def finish(s):
        rows = slice(s * sub, (s + 1) * sub)
        o_ref[rows, :] = _layer_norm(alpha * h_ref[rows, :] + 0.5 * acc[s], g_ref[...], b_ref[...])

    nxt = up(*items[0])
    ready = None
    for idx, (s, j) in enumerate(items):
        a, b = nxt
        if idx + 1 < len(items):
            nxt = up(*items[idx + 1])
        if ready is not None:
            finish(ready)
            ready = None
        y = _dot((_silu(a) * b).astype(BF16), wdn_ref[cuts[j]:cuts[j + 1], :])
        acc[s] = y if acc[s] is None else acc[s] + y
        if j == n_ch - 1:
            ready = s
    finish(ready)


def _ffn_ln(h, w_up, w_down, layer, g, b, alpha):
    n, d = h.shape
    d_ff = w_down.shape[1]
    cuts = tuple(range(0, d_ff, FFN_CHUNK)) + (d_ff,)
    tm = _row_tile(n, ROW_TILE)
    sub = ROW_SUB if tm % ROW_SUB == 0 else tm
    return pl.pallas_call(
        functools.partial(_ffn_ln_kernel, d_ff=d_ff, cuts=cuts, sub=sub, alpha=alpha),
        grid=(n // tm,),
        in_specs=[
            pl.BlockSpec((tm, d), lambda i: (i, 0)),
            _layer_of(w_up, layer),
            _layer_of(w_down, layer),
            _resident(g.shape),
            _resident(b.shape),
        ],
        out_specs=pl.BlockSpec((tm, d), lambda i: (i, 0)),
        out_shape=jax.ShapeDtypeStruct((n, d), F32),
        compiler_params=pltpu.CompilerParams(
            dimension_semantics=("parallel",), vmem_limit_bytes=VMEM_LIMIT),
        name="ffn_ln",
    )(h, w_up, w_down, g, b)


def _head_norm(o, gain):
    mu = jnp.mean(o, axis=-1, keepdims=True)
    oc = o - mu
    var = jnp.mean(oc * oc, axis=-1, keepdims=True)
    return oc * lax.rsqrt(var + GN_EPS) * gain


def _attn_kernel(*refs, nseq, stagger, blk, dblk, dk, dv, masked, alpha):
    n_in = 24 if masked else 23
    (h_ref, wmix_ref, wag_ref, wa2_ref, ba_ref, cos_ref, sin_ref, decq_ref, deck_ref, tri_ref,
     lvl_ref, gla0_ref, ret0_ref, gna_ref, gnr_ref, gpow_ref, wg_ref, bm_ref, woa_ref, wob_ref,
     wo_ref, lng_ref, lnb_ref) = refs[:23]
    valid_ref = refs[23] if masked else None
    o_ref, sg_ref, sr_ref, sgt_scr, sr_scr, b_scr, oa_scr, ob_scr, gate_scr = refs[n_in:]
    hk = N_HEADS * dk
    hv = N_HEADS * dv
    d = h_ref.shape[-1]
    n_diag = blk // dblk
    lv_diag = dblk.bit_length() - 1
    n_lv = blk.bit_length() - 1
    c0 = 2 * hk + hv
    pw = d // 2
    n_gate = 4 * d // pw
    half = blk // 2 if blk % (2 * SUBLANES) == 0 else blk
    j = pl.program_id(1)

    def shared(ref, s):
        return min(s, ref.shape[0] - 1)

    @pl.when(j == 0)
    def _():
        for s in range(nseq):
            for hd in range(N_HEADS):
                sgt_scr[s, hd] = gla0_ref[shared(gla0_ref, s), hd].T
            sr_scr[s] = ret0_ref[shared(ret0_ref, s)]

    lvl = lvl_ref[...]
    r8 = lax.broadcasted_iota(jnp.int32, (blk, hk), 0) & (SUBLANES - 1)
    causal = (lax.broadcasted_iota(jnp.int32, (blk, blk), 0)
              >= lax.broadcasted_iota(jnp.int32, (blk, blk), 1))

    def tile(x, i, hd):
        return x[i * dblk:(i + 1) * dblk, hd * dk:(hd + 1) * dk]

    def sequence(s):
        h = h_ref[s]
        hb = h.astype(BF16)
        ag = _dot(hb, wag_ref[...])
        qg = _dot(hb, wmix_ref[:, 0:hk]) * (dk ** -0.5)
        yield
        lg = _log_sigmoid(_dot(ag.astype(BF16), wa2_ref[...]) + ba_ref[...]) / GATE_TAU
        kg = _dot(hb, wmix_ref[:, hk:2 * hk])
        if masked:
            valid = valid_ref[s]
            vmask = jnp.concatenate([valid] * N_HEADS, axis=1)
            lg = lg * vmask
            kg = kg * vmask
        yield
        lg_hi = lg.astype(BF16)
        rem = lg - lg_hi.astype(F32)
        lg_mid = rem.astype(BF16)
        lg_lo = (rem - lg_mid.astype(F32)).astype(BF16)
        tri = tri_ref[...]
        b2 = (_dot(tri, lg_hi) + _dot(tri, lg_mid) + _dot(tri, lg_lo)) * LOG2E
        b_scr[s] = b2
        yield

        def in_group_rows(r):
            return jnp.concatenate(
                [jnp.broadcast_to(b_scr[s, pl.ds(SUBLANES * i + r, 1), :], (SUBLANES, hk))
                 for i in range(blk // SUBLANES)], axis=0)

        def level_factor(lv):
            if lv == 0:
                a = jnp.where(r8 < 2, in_group_rows(0),
                              jnp.where(r8 < 4, in_group_rows(2),
                                        jnp.where(r8 < 6, in_group_rows(4), in_group_rows(6))))
                diff = b2 - a
            elif lv == 1:
                diff = b2 - jnp.where(r8 < 4, in_group_rows(1), in_group_rows(5))
            elif lv == 2:
                diff = b2 - in_group_rows(3)
            else:
                grp = 2 << lv
                diff = jnp.concatenate(
                    [b2[g * grp:(g + 1) * grp] - b_scr[s, pl.ds(g * grp + grp // 2 - 1, 1), :]
                     for g in range(blk // grp)], axis=0)
            return jnp.exp2(-jnp.abs(diff))

        proj = [lambda: _dot(hb, wmix_ref[:, 2 * hk:2 * hk + hv]).astype(BF16),
                lambda: _dot(hb, wmix_ref[:, c0:c0 + hk]),
                lambda: _dot(hb, wmix_ref[:, c0 + hk:c0 + 2 * hk]),
                lambda: _dot(hb, wmix_ref[:, c0 + 2 * hk:c0 + 2 * hk + hv]).astype(BF16)]
        fac = []
        done = []
        for lv in range(n_lv):
            if 2 * len(done) <= lv and len(done) < len(proj):
                done.append(proj[len(done)]())
            fac.append(level_factor(lv))
            yield
        while len(done) < len(proj):
            done.append(proj[len(done)]())
        vg, qr, kr, vr = done
        last = b_scr[s, pl.ds(blk - 1, 1), :]
        q_in = (qg * jnp.exp2(b2)).astype(BF16)
        k_out = (kg * jnp.exp2(-jnp.abs(last - b2))).astype(BF16)
        s_dec = jnp.exp2(last)
        yield

        gates = []

        def gate_piece():
            p = len(gates)
            x = _dot(hb, wg_ref[:, p * pw:(p + 1) * pw])
            if p * pw < 2 * d:
                act = _silu(x)
            else:
                act = _sigmoid(x + bm_ref[:, p * pw - 2 * d:(p + 1) * pw - 2 * d])
            gate_scr[s, :, p * pw:(p + 1) * pw] = act
            gates.append(p)

        qgb = qg.astype(BF16)
        kgb = kg.astype(BF16)
        on_diag = lvl == lv_diag
        att = [[jnp.where(on_diag, _dot_nt(tile(qgb, i, hd), tile(kgb, i, hd)), 0.0)
                for hd in range(N_HEADS)] for i in range(n_diag)]
        yield
        for lv in range(lv_diag):
            qf = (qg * fac[lv]).astype(BF16)
            kf = (kg * fac[lv]).astype(BF16)
            at_level = lvl == lv
            for i in range(n_diag):
                for hd in range(N_HEADS):
                    att[i][hd] = jnp.where(at_level, _dot_nt(tile(qf, i, hd), tile(kf, i, hd)), att[i][hd])
            if len(gates) < n_gate:
                gate_piece()
            yield
        while len(gates) < n_gate:
            gate_piece()
            yield
        if n_diag == 2:
            qf = (qg[dblk:] * fac[lv_diag][dblk:]).astype(BF16)
            kf = (kg[:dblk] * fac[lv_diag][:dblk]).astype(BF16)
        gna = gna_ref[...]
        for hd in range(N_HEADS):
            ks = slice(hd * dk, (hd + 1) * dk)
            vs = slice(hd * dv, (hd + 1) * dv)
            v = vg[:, vs]
            if n_diag == 2:
                below = _dot_nt(qf[:, ks], kf[:, ks])
                o = jnp.concatenate([
                    _dot(att[0][hd].astype(BF16), v[:dblk]),
                    _dot(jnp.concatenate([below, att[1][hd]], axis=1).astype(BF16), v)], axis=0)
            else:
                o = _dot(att[0][hd].astype(BF16), v)
            o = o + _dot_nt(q_in[:, ks], sgt_scr[s, hd].astype(BF16))
            oa_scr[s, :, vs] = _head_norm(o, gna[:, vs])
            yield

        cos = cos_ref[shared(cos_ref, s)]
        sin = sin_ref[shared(sin_ref, s)]
        decq = decq_ref[...]
        deck = deck_ref[...]
        gpow = gpow_ref[shared(gpow_ref, s)]
        gnr = gnr_ref[...]
        ret_kb = []
        for hd in range(N_HEADS):
            ks = slice(hd * dk, (hd + 1) * dk)
            vs = slice(hd * dv, (hd + 1) * dv)
            q = qr[:, ks]
            k = kr[:, ks]
            q = (q * cos + pltpu.roll(q, dk // 2, 1) * sin) * decq[:, ks]
            k = (k * cos + pltpu.roll(k, dk // 2, 1) * sin) * deck[:, ks]
            if masked:
                k = k * valid
            qb = q.astype(BF16)
            kb = k.astype(BF16)
            v = vr[:, vs]
            sc = jnp.where(causal, _dot_nt(qb, kb), 0.0)
            o = _dot(sc.astype(BF16), v) + _dot(qb, sr_scr[s, hd].astype(BF16))
            ret_kb.append(kb)
            ob_scr[s, :, vs] = _head_norm(o, gnr[:, vs])
            yield

        y = []
        for r0 in range(0, blk, half):
            rs = slice(r0, r0 + half)
            ya = _dot((gate_scr[s, rs, 0:d] * oa_scr[s, rs, :]).astype(BF16), woa_ref[...])
            yb = _dot((gate_scr[s, rs, d:2 * d] * ob_scr[s, rs, :]).astype(BF16), wob_ref[...])
            merged = gate_scr[s, rs, 2 * d:3 * d] * ya + gate_scr[s, rs, 3 * d:4 * d] * yb
            y.append(_dot(merged.astype(BF16), wo_ref[...]))
            yield
        for i, r0 in enumerate(range(0, blk, half)):
            rs = slice(r0, r0 + half)
            if r0 + half == blk:
                for hd in range(N_HEADS):
                    ks = slice(hd * dk, (hd + 1) * dk)
                    vs = slice(hd * dv, (hd + 1) * dv)
                    sgt_scr[s, hd] = sgt_scr[s, hd] * s_dec[:, ks] + _dot_tn(vg[:, vs], k_out[:, ks])
                    sr_scr[s, hd] = (sr_scr[s, hd] + _dot_tn(ret_kb[hd], vr[:, vs])) * gpow[:, vs]
            o_ref[s, rs, :] = _layer_norm(alpha * h[rs] + y[i], lng_ref[...], lnb_ref[...])
            yield

    waiting = [sequence(s) for s in range(nseq)]
    running = []
    tick = 0
    while waiting or running:
        if waiting and tick % stagger == 0:
            running.append(waiting.pop(0))
        for g in list(running):
            try:
                next(g)
            except StopIteration:
                running.remove(g)
        tick += 1

    @pl.when(j == pl.num_programs(1) - 1)
    def _():
        for s in range(nseq):
            for hd in range(N_HEADS):
                sg_ref[s, hd] = sgt_scr[s, hd].T
            sr_ref[s] = sr_scr[s]


def _level_matrix(n):
    r = np.arange(n)[:, None]
    c = np.arange(n)[None, :]
    x = np.maximum(r ^ c, 1)
    lvl = np.where(r > c, np.floor(np.log2(x)), np.where(r == c, n.bit_length() - 1, -1))
    return lvl.astype(np.int32)


def _attn(h, gla0, ret0, w, layer, vec, tabs, blk, nseq, alpha, valid=None):
    nb, t, d = h.shape
    _, _, dk, dv = gla0.shape
    hk, hv = N_HEADS * dk, N_HEADS * dv
    dblk = min(blk, DIAG_BLOCK)
    assert t % blk == 0 and blk & (blk - 1) == 0 and SUBLANES <= blk <= 256 and blk // dblk <= 2
    assert nb % nseq == 0
    lvl = jnp.asarray(_level_matrix(dblk))
    tri = jnp.asarray(np.tril(np.ones((blk, blk), np.float32)), BF16)
    masked = valid is not None

    def per_seq(a, width):
        if a.shape[0] > 1:
            return pl.BlockSpec((nseq, blk, width), lambda b, j: (b, j, 0))
        return pl.BlockSpec((1, blk, width), lambda b, j: (0, j, 0))

    def state_in(a):
        if a.shape[0] > 1:
            return pl.BlockSpec((nseq, N_HEADS, dk, dv), lambda b, j: (b, 0, 0, 0))
        return pl.BlockSpec((1, N_HEADS, dk, dv), lambda b, j: (0, 0, 0, 0))

    seq = pl.BlockSpec((nseq, blk, d), lambda b, j: (b, j, 0))
    args = [h, w["mix"], w["ag"], w["a2"], vec["ba"], tabs["cos"], tabs["sin"], tabs["decq"], tabs["deck"],
            tri, lvl, gla0, ret0, vec["gna"], vec["gnr"], tabs["gpow"], w["g"], vec["bm"], w["oa"], w["ob"],
            w["o"], vec["lng"], vec["lnb"]]
    in_specs = [
        seq,
        _layer_of(w["mix"], layer), _layer_of(w["ag"], layer), _layer_of(w["a2"], layer),
        _resident(vec["ba"].shape),
        per_seq(tabs["cos"], dk), per_seq(tabs["sin"], dk),
        _resident(tabs["decq"].shape), _resident(tabs["deck"].shape),
        _resident(tri.shape), _resident(lvl.shape),
        state_in(gla0), state_in(ret0),
        _resident(vec["gna"].shape), _resident(vec["gnr"].shape),
        (pl.BlockSpec((nseq, 1, hv), lambda b, j: (b, 0, 0)) if tabs["gpow"].shape[0] > 1
         else pl.BlockSpec((1, 1, hv), lambda b, j: (0, 0, 0))),
        _layer_of(w["g"], layer), _resident(vec["bm"].shape), _layer_of(w["oa"], layer),
        _layer_of(w["ob"], layer), _layer_of(w["o"], layer), _resident(vec["lng"].shape),
        _resident(vec["lnb"].shape),
    ]
    if masked:
        in_specs.append(pl.BlockSpec((nseq, blk, dk), lambda b, j: (b, j, 0)))
        args.append(valid)
    state_out = pl.BlockSpec((nseq, N_HEADS, dk, dv), lambda b, j: (b, 0, 0, 0))
    return pl.pallas_call(
        functools.partial(_attn_kernel, nseq=nseq, stagger=SEQ_STAGGER, blk=blk, dblk=dblk, dk=dk, dv=dv,
                          masked=masked, alpha=alpha),
        grid=(nb // nseq, t // blk),
        in_specs=in_specs,
        out_specs=[seq, state_out, state_out],
        out_shape=[jax.ShapeDtypeStruct((nb, t, d), F32),
                   jax.ShapeDtypeStruct((nb, N_HEADS, dk, dv), F32),
                   jax.ShapeDtypeStruct((nb, N_HEADS, dk, dv), F32)],
        scratch_shapes=[pltpu.VMEM((nseq, N_HEADS, dv, dk), F32), pltpu.VMEM((nseq, N_HEADS, dk, dv), F32),
                        pltpu.VMEM((nseq, blk, hk), F32), pltpu.VMEM((nseq, blk, hv), F32),
                        pltpu.VMEM((nseq, blk, hv), F32), pltpu.VMEM((nseq, blk, 4 * d), F32)],
        compiler_params=pltpu.CompilerParams(
            dimension_semantics=("parallel", "arbitrary"), vmem_limit_bytes=VMEM_LIMIT),
        name="attn",
    )(*args)


def _mix_tables(pos, blk, n_valid, dk, dv):
    half = dk // 2
    inv = ROPE_BASE ** (-jnp.arange(half, dtype=F32) / half)
    ang = pos.astype(F32)[..., None] * inv
    cos = jnp.cos(ang)
    sin = jnp.sin(ang)
    log_gamma = jnp.log1p(-(2.0 ** (-5.0 - jnp.arange(N_HEADS, dtype=F32))))
    step = jnp.arange(blk, dtype=F32)[:, None, None] + 1.0
    dec = jnp.broadcast_to(step * log_gamma[None, :, None], (blk, N_HEADS, dk)).reshape(blk, N_HEADS * dk)
    gpow = jnp.exp(n_valid.astype(F32)[:, None, None] * log_gamma[None, :, None])
    gpow = jnp.broadcast_to(gpow, (pos.shape[0], N_HEADS, dv)).reshape(pos.shape[0], 1, N_HEADS * dv)
    return {
        "cos": jnp.concatenate([cos, cos], axis=-1),
        "sin": jnp.concatenate([-sin, sin], axis=-1),
        "decq": jnp.exp(dec),
        "deck": jnp.exp(-dec) * dk ** -0.5,
        "gpow": gpow,
    }


def kernel(x_prompt, x_sample, state_gla, state_ret, meta, ln_g, ln_b, w_ffn1_up, w_ffn1_down, w_in,
           w_alpha2, b_alpha, b_merge, gn_gla, gn_ret, w_o_gla, w_o_ret, w_out, w_ffn2_up, w_ffn2_down):
    nb, seq, d = x_prompt.shape
    nbs, seq_s, _ = x_sample.shape
    depth = ln_g.shape[0]
    n_meta = meta.shape[0]
    _, _, _, dk, dv = state_gla.shape
    hk, hv = N_HEADS * dk, N_HEADS * dv
    rank = w_alpha2.shape[1]
    alpha = (2.0 * depth) ** 0.25
    blk = min(MAIN_BLOCK, seq)
    sblk = SMALL_BLOCK
    nseq = MAIN_SEQS if nb % MAIN_SEQS == 0 else 1
    assert n_meta <= sblk and seq_s <= sblk

    hs = jnp.zeros((1 + nbs, sblk, d), F32)
    hs = hs.at[0, :n_meta].set(meta.astype(F32)).at[1:, :seq_s].set(x_sample)
    n_valid_s = jnp.asarray([n_meta] + [seq_s] * nbs, jnp.int32)
    valid_s = (jnp.arange(sblk)[None, :, None] < n_valid_s[:, None, None]).astype(F32)
    valid_s = jnp.broadcast_to(valid_s, (1 + nbs, sblk, dk))
    off = jnp.asarray([0] + [n_meta + PAST_LEN] * nbs, jnp.int32)
    tabs_s = _mix_tables(off[:, None] + jnp.arange(sblk)[None, :], sblk, n_valid_s, dk, dv)
    tabs_p = _mix_tables(n_meta + jnp.arange(seq)[None, :], blk, jnp.asarray([blk], jnp.int32), dk, dv)
    hp = x_prompt

    c_rg = 2 * hk + hv
    c_ag = c_rg + hv
    c_qr = c_ag + rank
    c_gr = c_qr + 2 * hk + hv
    c_mg = c_gr + hv
    w = {
        "mix": jnp.concatenate([w_in[:, :, :c_rg], w_in[:, :, c_qr:c_gr]], axis=2).astype(BF16),
        "ag": jnp.pad(w_in[:, :, c_ag:c_qr], ((0, 0), (0, 0), (0, GATE_RANK_PAD - rank))).astype(BF16),
        "a2": jnp.pad(w_alpha2, ((0, 0), (0, GATE_RANK_PAD - rank), (0, 0))).astype(BF16),
        "g": jnp.concatenate([w_in[:, :, c_rg:c_ag], w_in[:, :, c_gr:c_mg], w_in[:, :, c_mg:]],
                             axis=2).astype(BF16),
        "oa": w_o_gla.astype(BF16),
        "ob": w_o_ret.astype(BF16),
        "o": w_out.astype(BF16),
    }
    w1u, w1d = w_ffn1_up.astype(BF16), w_ffn1_down.astype(BF16)
    w2u, w2d = w_ffn2_up.astype(BF16), w_ffn2_down.astype(BF16)

    gla_p, ret_p, gla_s, ret_s = [], [], [], []
    for l in range(depth):
        vec = {
            "ba": b_alpha[l][None, :],
            "gna": gn_gla[l][None, :],
            "gnr": gn_ret[l][None, :],
            "bm": b_merge[l][None, :],
            "lng": ln_g[l, 1][None, :],
            "lnb": ln_b[l, 1][None, :],
        }

        def layer(h3, gla0, ret0, tabs, bsz, ns, valid):
            shape = h3.shape
            h = _ffn_ln(h3.reshape(-1, d), w1u, w1d, l, ln_g[l, 0][None, :], ln_b[l, 0][None, :], alpha)
            h, sg, sr = _attn(h.reshape(shape), gla0, ret0, w, l, vec, tabs, bsz, ns, alpha, valid)
            h = _ffn_ln(h.reshape(-1, d), w2u, w2d, l, ln_g[l, 2][None, :], ln_b[l, 2][None, :], alpha)
            return h.reshape(shape), sg, sr

        zero = jnp.zeros((1, N_HEADS, dk, dv), F32)
        hs, sg_s, sr_s = layer(hs, jnp.concatenate([zero, state_gla[l].astype(F32)], axis=0),
                               jnp.concatenate([zero, state_ret[l].astype(F32)], axis=0),
                               tabs_s, sblk, 1, valid_s)
        hp, sg_p, sr_p = layer(hp, sg_s[:1], sr_s[:1], tabs_p, blk, nseq, None)
        gla_p.append(sg_p.astype(x_prompt.dtype))
        ret_p.append(sr_p.astype(x_prompt.dtype))
        gla_s.append(sg_s[1:].astype(state_gla.dtype))
        ret_s.append(sr_s[1:].astype(state_ret.dtype))

    return (hp, hs[1:, :seq_s], jnp.stack(gla_p, axis=0), jnp.stack(ret_p, axis=0),
            jnp.stack(gla_s, axis=0), jnp.stack(ret_s, axis=0))
```

```python
import functools

import numpy as np
import jax
import jax.numpy as jnp
from jax import lax
from jax.experimental import pallas as pl
from jax.experimental.pallas import tpu as pltpu

F32 = jnp.float32
BF16 = jnp.bfloat16

N_HEADS = 4
GATE_RANK_PAD = 128
GATE_TAU = 16.0
PAST_LEN = 1024
ROPE_BASE = 10000.0
LN_EPS = 1e-5
GN_EPS = 1e-5
LOG2E = 1.4426950408889634

SUBLANES = 8
MXU_WIDTH = 256
MAIN_BLOCK = 256
MAIN_SEQS = 1
MAIN_SUBBLOCKS = 2
STAGGER = 26
DIAG_BLOCK = 128
GATE_SLOTS = (1, 1, (1, 0, 1, 0, 1, 0, 1, 0), 0, (1, 1, 0, 0), 0)
SMALL_BLOCK = 32
ROW_TILE = 1024
ROW_SUB = 256
FFN_CHUNK = 6 * MXU_WIDTH
VMEM_LIMIT = 56 * 1024 * 1024

_NT = (((1,), (1,)), ((), ()))
_TN = (((0,), (0,)), ((), ()))


def _dot(a, b):
    return jnp.dot(a, b, preferred_element_type=F32)


def _dot_nt(a, b):
    return lax.dot_general(a, b, _NT, preferred_element_type=F32)


def _dot_tn(a, b):
    return lax.dot_general(a, b, _TN, preferred_element_type=F32)


def _layer_norm(x, g, b):
    mu = jnp.mean(x, axis=-1, keepdims=True)
    xc = x - mu
    var = jnp.mean(xc * xc, axis=-1, keepdims=True)
    return xc * lax.rsqrt(var + LN_EPS) * g + b


def _sigmoid(x):
    return 0.5 + 0.5 * jnp.tanh(0.5 * x)


def _silu(x):
    hx = 0.5 * x
    return hx + hx * jnp.tanh(hx)


def _log_sigmoid(x):
    return jnp.minimum(x, 0.0) - jnp.log1p(jnp.exp(-jnp.abs(x)))


def _resident(shape):
    return pl.BlockSpec(shape, lambda *_: (0,) * len(shape), pipeline_mode=pl.Buffered(1))


def _layer_of(stacked):
    rest = stacked.shape[1:]
    return pl.BlockSpec((None,) + rest, lambda *a: (a[-1][0],) + (0,) * len(rest),
                        pipeline_mode=pl.Buffered(1))


def _row_tile(n_rows, limit):
    t = min(limit, n_rows)
    while n_rows % t or t % SUBLANES:
        t -= SUBLANES
    return t


def _ffn_ln_kernel(_layer_ref, h_ref, wup_ref, wdn_ref, g_ref, b_ref, o_ref, *, d_ff, cuts, sub, alpha):
    n_sub = h_ref.shape[0] // sub
    n_ch = len(cuts) - 1
    items = [(s, j) for s in range(n_sub) for j in range(n_ch)]
    hb = [None] * n_sub
    acc = [None] * n_sub

    def up(s, j):
        if hb[s] is None:
            hb[s] = h_ref[s * sub:(s + 1) * sub, :].astype(BF16)
        return (_dot(hb[s], wup_ref[:, cuts[j]:cuts[j + 1]]),
                _dot(hb[s], wup_ref[:, d_ff + cuts[j]:d_ff + cuts[j + 1]]))

    def finish(s):
        rows = slice(s * sub, (s + 1) * sub)
        o_ref[rows, :] = _layer_norm(alpha * h_ref[rows, :] + 0.5 * acc[s], g_ref[...], b_ref[...])

    nxt = up(*items[0])
    ready = None
    for idx, (s, j) in enumerate(items):
        a, b = nxt
        if idx + 1 < len(items):
            nxt = up(*items[idx + 1])
        if ready is not None:
            finish(ready)
            ready = None
        y = _dot((_silu(a) * b).astype(BF16), wdn_ref[cuts[j]:cuts[j + 1], :])
        acc[s] = y if acc[s] is None else acc[s] + y
        if j == n_ch - 1:
            ready = s
    finish(ready)


def _ffn_ln(h, w_up, w_down, layer, g, b, alpha):
    n, d = h.shape
    d_ff = w_down.shape[1]
    cuts = tuple(range(0, d_ff, FFN_CHUNK)) + (d_ff,)
    tm = _row_tile(n, ROW_TILE)
    sub = ROW_SUB if tm % ROW_SUB == 0 else tm
    return pl.pallas_call(
        functools.partial(_ffn_ln_kernel, d_ff=d_ff, cuts=cuts, sub=sub, alpha=alpha),
        grid_spec=pltpu.PrefetchScalarGridSpec(
            num_scalar_prefetch=1,
            grid=(n // tm,),
            in_specs=[
                pl.BlockSpec((tm, d), lambda i, l: (i, 0)),
                _layer_of(w_up),
                _layer_of(w_down),
                _resident(g.shape),
                _resident(b.shape),
            ],
            out_specs=pl.BlockSpec((tm, d), lambda i, l: (i, 0)),
        ),
        out_shape=jax.ShapeDtypeStruct((n, d), F32),
        compiler_params=pltpu.CompilerParams(
            dimension_semantics=("parallel",), vmem_limit_bytes=VMEM_LIMIT),
        name="ffn_ln",
    )(jnp.asarray([layer], jnp.int32), h, w_up, w_down, g, b)


def _head_norm(o, gain):
    mu = jnp.mean(o, axis=-1, keepdims=True)
    oc = o - mu
    var = jnp.mean(oc * oc, axis=-1, keepdims=True)
    return oc * lax.rsqrt(var + GN_EPS) * gain


def _attn_kernel(_layer_ref, *refs, nseq, nsub, stagger, blk, dblk, dk, dv, masked, alpha):
    n_in = 24 if masked else 23
    (h_ref, wmix_ref, wag_ref, wa2_ref, ba_ref, cos_ref, sin_ref, decq_ref, deck_ref, tri_ref,
     lvl_ref, gla0_ref, ret0_ref, gna_ref, gnr_ref, gpow_ref, wg_ref, bm_ref, woa_ref, wob_ref,
     wo_ref, lng_ref, lnb_ref) = refs[:23]
    valid_ref = refs[23] if masked else None
    o_ref, sg_ref, sr_ref, sgt_scr, sr_scr, b_scr, oa_scr, ob_scr = refs[n_in:]
    hk = N_HEADS * dk
    hv = N_HEADS * dv
    d = h_ref.shape[-1]
    n_diag = blk // dblk
    lv_diag = dblk.bit_length() - 1
    n_lv = blk.bit_length() - 1
    c0 = 2 * hk + hv
    pw = d // 2
    n_gate = 4 * d // pw
    j = pl.program_id(1)

    def shared(ref, s):
        return min(s, ref.shape[0] - 1)

    @pl.when(j == 0)
    def _():
        for s in range(nseq):
            for hd in range(N_HEADS):
                sgt_scr[s, hd] = gla0_ref[shared(gla0_ref, s), hd].T
            sr_scr[s] = ret0_ref[shared(ret0_ref, s)]

    lvl = lvl_ref[...]
    r8 = lax.broadcasted_iota(jnp.int32, (blk, hk), 0) & (SUBLANES - 1)
    causal = (lax.broadcasted_iota(jnp.int32, (blk, blk), 0)
              >= lax.broadcasted_iota(jnp.int32, (blk, blk), 1))

    def tile(x, i, hd):
        return x[i * dblk:(i + 1) * dblk, hd * dk:(hd + 1) * dk]

    def block(s, u):
        rows = slice(u * blk, (u + 1) * blk)
        su = s * nsub + u
        h = h_ref[s, rows, :]
        hb = h.astype(BF16)
        gates = []

        def gate_pieces(n):
            for _ in range(n):
                p = len(gates)
                if p == n_gate:
                    return
                x = _dot(hb, wg_ref[:, p * pw:(p + 1) * pw])
                if p * pw < 2 * d:
                    gates.append(_silu(x))
                else:
                    gates.append(_sigmoid(x + bm_ref[:, p * pw - 2 * d:(p + 1) * pw - 2 * d]))

        def slot(k, i):
            n = GATE_SLOTS[k]
            return n if isinstance(n, int) else (n[i] if i < len(n) else 0)

        ag = _dot(hb, wag_ref[...])
        qg = _dot(hb, wmix_ref[:, 0:hk]) * (dk ** -0.5)
        yield
        gate_in = _dot(ag.astype(BF16), wa2_ref[...])
        kg = _dot(hb, wmix_ref[:, hk:2 * hk])
        vg = _dot(hb, wmix_ref[:, 2 * hk:2 * hk + hv]).astype(BF16)
        gate_pieces(slot(0, 0))
        lg = _log_sigmoid(gate_in + ba_ref[...]) / GATE_TAU
        if masked:
            valid = valid_ref[s, rows, :]
            vmask = jnp.concatenate([valid] * N_HEADS, axis=1)
            lg = lg * vmask
            kg = kg * vmask
        yield
        qr = _dot(hb, wmix_ref[:, c0:c0 + hk])
        kr = _dot(hb, wmix_ref[:, c0 + hk:c0 + 2 * hk])
        lg_hi = lg.astype(BF16)
        rem = lg - lg_hi.astype(F32)
        lg_mid = rem.astype(BF16)
        lg_lo = (rem - lg_mid.astype(F32)).astype(BF16)
        tri = tri_ref[...]
        b2 = (_dot(tri, lg_hi) + _dot(tri, lg_mid) + _dot(tri, lg_lo)) * LOG2E
        b_scr[su] = b2
        vr = _dot(hb, wmix_ref[:, c0 + 2 * hk:c0 + 2 * hk + hv]).astype(BF16)
        gate_pieces(slot(1, 0))
        yield

        def in_group_rows(r):
            return jnp.concatenate(
                [jnp.broadcast_to(b_scr[su, pl.ds(SUBLANES * i + r, 1), :], (SUBLANES, hk))
                 for i in range(blk // SUBLANES)], axis=0)

        def level_factor(lv):
            if lv == 0:
                a = jnp.where(r8 < 2, in_group_rows(0),
                              jnp.where(r8 < 4, in_group_rows(2),
                                        jnp.where(r8 < 6, in_group_rows(4), in_group_rows(6))))
                diff = b2 - a
            elif lv == 1:
                diff = b2 - jnp.where(r8 < 4, in_group_rows(1), in_group_rows(5))
            elif lv == 2:
                diff = b2 - in_group_rows(3)
            else:
                grp = 2 << lv
                diff = jnp.concatenate(
                    [b2[g * grp:(g + 1) * grp] - b_scr[su, pl.ds(g * grp + grp // 2 - 1, 1), :]
                     for g in range(blk // grp)], axis=0)
            return jnp.exp2(-jnp.abs(diff))

        fac = []
        for lv in range(n_lv):
            gate_pieces(slot(2, lv))
            fac.append(level_factor(lv))
            yield
        last = b_scr[su, pl.ds(blk - 1, 1), :]
        q_in = (qg * jnp.exp2(b2)).astype(BF16)
        k_out = (kg * jnp.exp2(-jnp.abs(last - b2))).astype(BF16)
        s_dec = jnp.exp2(last)
        yield

        qgb = qg.astype(BF16)
        kgb = kg.astype(BF16)
        on_diag = lvl == lv_diag
        att = [[jnp.where(on_diag, _dot_nt(tile(qgb, i, hd), tile(kgb, i, hd)), 0.0)
                for hd in range(N_HEADS)] for i in range(n_diag)]
        yield
        for lv in range(lv_diag):
            qf = (qg * fac[lv]).astype(BF16)
            kf = (kg * fac[lv]).astype(BF16)
            at_level = lvl == lv
            for i in range(n_diag):
                for hd in range(N_HEADS):
                    att[i][hd] = jnp.where(at_level, _dot_nt(tile(qf, i, hd), tile(kf, i, hd)), att[i][hd])
            gate_pieces(slot(3, lv))
            yield
        if n_diag == 2:
            qf = (qg[dblk:] * fac[lv_diag][dblk:]).astype(BF16)
            kf = (kg[:dblk] * fac[lv_diag][:dblk]).astype(BF16)
        gna = gna_ref[...]
        for hd in range(N_HEADS):
            ks = slice(hd * dk, (hd + 1) * dk)
            vs = slice(hd * dv, (hd + 1) * dv)
            v = vg[:, vs]
            if n_diag == 2:
                below = _dot_nt(qf[:, ks], kf[:, ks])
                o = jnp.concatenate([
                    _dot(att[0][hd].astype(BF16), v[:dblk]),
                    _dot(jnp.concatenate([below, att[1][hd]], axis=1).astype(BF16), v)], axis=0)
            else:
                o = _dot(att[0][hd].astype(BF16), v)
            st = sgt_scr[s, hd]
            o = o + _dot_nt(q_in[:, ks], st.astype(BF16))
            sgt_scr[s, hd] = st * s_dec[:, ks] + _dot_tn(v, k_out[:, ks])
            oa_scr[su, :, vs] = _head_norm(o, gna[:, vs])
            gate_pieces(slot(4, hd))
            yield

        cos = cos_ref[shared(cos_ref, s), rows, :]
        sin = sin_ref[shared(sin_ref, s), rows, :]
        decq = decq_ref[...]
        deck = deck_ref[...]
        gpow = gpow_ref[shared(gpow_ref, s)]
        gnr = gnr_ref[...]
        for hd in range(N_HEADS):
            ks = slice(hd * dk, (hd + 1) * dk)
            vs = slice(hd * dv, (hd + 1) * dv)
            q = qr[:, ks]
            k = kr[:, ks]
            q = (q * cos + pltpu.roll(q, dk // 2, 1) * sin) * decq[:, ks]
            k = (k * cos + pltpu.roll(k, dk // 2, 1) * sin) * deck[:, ks]
            if masked:
                k = k * valid
            qb = q.astype(BF16)
            kb = k.astype(BF16)
            v = vr[:, vs]
            sc = jnp.where(causal, _dot_nt(qb, kb), 0.0)
            s0 = sr_scr[s, hd]
            o = _dot(sc.astype(BF16), v) + _dot(qb, s0.astype(BF16))
            sr_scr[s, hd] = (s0 + _dot_tn(kb, v)) * gpow[:, vs]
            ob_scr[su, :, vs] = _head_norm(o, gnr[:, vs])
            gate_pieces(slot(5, hd))
            yield

        gate_pieces(n_gate)
        rg, gr, ga, gb = [jnp.concatenate(gates[i * (d // pw):(i + 1) * (d // pw)], axis=1)
                          for i in range(4)]
        ya = _dot((rg * oa_scr[su]).astype(BF16), woa_ref[...])
        yb = _dot((gr * ob_scr[su]).astype(BF16), wob_ref[...])
        y = _dot((ga * ya + gb * yb).astype(BF16), wo_ref[...])
        o_ref[s, rows, :] = _layer_norm(alpha * h + y, lng_ref[...], lnb_ref[...])
        yield

    waiting = [block(s, u) for u in range(nsub) for s in range(nseq)]
    running = []
    tick = 0
    while waiting or running:
        if waiting and tick % stagger == 0:
            running.append(waiting.pop(0))
        for g in list(running):
            try:
                next(g)
            except StopIteration:
                running.remove(g)
        tick += 1

    @pl.when(j == pl.num_programs(1) - 1)
    def _():
        for s in range(nseq):
            for hd in range(N_HEADS):
                sg_ref[s, hd] = sgt_scr[s, hd].T
            sr_ref[s] = sr_scr[s]


def _level_matrix(n):
    r = np.arange(n)[:, None]
    c = np.arange(n)[None, :]
    x = np.maximum(r ^ c, 1)
    lvl = np.where(r > c, np.floor(np.log2(x)), np.where(r == c, n.bit_length() - 1, -1))
    return lvl.astype(np.int32)


def _attn(h, gla0, ret0, w, layer, vec, tabs, blk, nseq, alpha, valid=None):
    nb, t, d = h.shape
    _, _, dk, dv = gla0.shape
    hk, hv = N_HEADS * dk, N_HEADS * dv
    dblk = min(blk, DIAG_BLOCK)
    nsub = MAIN_SUBBLOCKS if (valid is None and t % (MAIN_SUBBLOCKS * blk) == 0) else 1
    tstep = nsub * blk
    assert t % tstep == 0 and blk & (blk - 1) == 0 and SUBLANES <= blk <= 256 and blk // dblk <= 2
    assert nb % nseq == 0
    lvl = jnp.asarray(_level_matrix(dblk))
    tri = jnp.asarray(np.tril(np.ones((blk, blk), np.float32)), BF16)
    masked = valid is not None

    def per_seq(a, width):
        if a.shape[0] > 1:
            return pl.BlockSpec((nseq, tstep, width), lambda b, j, l: (b, j, 0))
        return pl.BlockSpec((1, tstep, width), lambda b, j, l: (0, j, 0))

    def state_in(a):
        if a.shape[0] > 1:
            return pl.BlockSpec((nseq, N_HEADS, dk, dv), lambda b, j, l: (b, 0, 0, 0))
        return pl.BlockSpec((1, N_HEADS, dk, dv), lambda b, j, l: (0, 0, 0, 0))

    seq = pl.BlockSpec((nseq, tstep, d), lambda b, j, l: (b, j, 0))
    args = [h, w["mix"], w["ag"], w["a2"], vec["ba"], tabs["cos"], tabs["sin"], tabs["decq"], tabs["deck"],
            tri, lvl, gla0, ret0, vec["gna"], vec["gnr"], tabs["gpow"], w["g"], vec["bm"], w["oa"], w["ob"],
            w["o"], vec["lng"], vec["lnb"]]
    in_specs = [
        seq,
        _layer_of(w["mix"]), _layer_of(w["ag"]), _layer_of(w["a2"]),
        _resident(vec["ba"].shape),
        per_seq(tabs["cos"], dk), per_seq(tabs["sin"], dk),
        _resident(tabs["decq"].shape), _resident(tabs["deck"].shape),
        _resident(tri.shape), _resident(lvl.shape),
        state_in(gla0), state_in(ret0),
        _resident(vec["gna"].shape), _resident(vec["gnr"].shape),
        (pl.BlockSpec((nseq, 1, hv), lambda b, j, l: (b, 0, 0)) if tabs["gpow"].shape[0] > 1
         else pl.BlockSpec((1, 1, hv), lambda b, j, l: (0, 0, 0))),
        _layer_of(w["g"]), _resident(vec["bm"].shape), _layer_of(w["oa"]),
        _layer_of(w["ob"]), _layer_of(w["o"]), _resident(vec["lng"].shape),
        _resident(vec["lnb"].shape),
    ]
    if masked:
        in_specs.append(pl.BlockSpec((nseq, tstep, dk), lambda b, j, l: (b, j, 0)))
        args.append(valid)
    state_out = pl.BlockSpec((nseq, N_HEADS, dk, dv), lambda b, j, l: (b, 0, 0, 0))
    return pl.pallas_call(
        functools.partial(_attn_kernel, nseq=nseq, nsub=nsub, stagger=STAGGER, blk=blk, dblk=dblk, dk=dk, dv=dv,
                          masked=masked, alpha=alpha),
        grid_spec=pltpu.PrefetchScalarGridSpec(
            num_scalar_prefetch=1,
            grid=(nb // nseq, t // tstep),
            in_specs=in_specs,
            out_specs=[seq, state_out, state_out],
            scratch_shapes=[
                pltpu.VMEM((nseq, N_HEADS, dv, dk), F32), pltpu.VMEM((nseq, N_HEADS, dk, dv), F32),
                pltpu.VMEM((nseq * nsub, blk, hk), F32), pltpu.VMEM((nseq * nsub, blk, hv), F32),
                pltpu.VMEM((nseq * nsub, blk, hv), F32)],
        ),
        out_shape=[jax.ShapeDtypeStruct((nb, t, d), F32),
                   jax.ShapeDtypeStruct((nb, N_HEADS, dk, dv), F32),
                   jax.ShapeDtypeStruct((nb, N_HEADS, dk, dv), F32)],
        compiler_params=pltpu.CompilerParams(
            dimension_semantics=("parallel", "arbitrary"), vmem_limit_bytes=VMEM_LIMIT),
        name="attn",
    )(jnp.asarray([layer], jnp.int32), *args)


def _mix_tables(pos, blk, n_valid, dk, dv):
    half = dk // 2
    inv = ROPE_BASE ** (-jnp.arange(half, dtype=F32) / half)
    ang = pos.astype(F32)[..., None] * inv
    cos = jnp.cos(ang)
    sin = jnp.sin(ang)
    log_gamma = jnp.log1p(-(2.0 ** (-5.0 - jnp.arange(N_HEADS, dtype=F32))))
    step = jnp.arange(blk, dtype=F32)[:, None, None] + 1.0
    dec = jnp.broadcast_to(step * log_gamma[None, :, None], (blk, N_HEADS, dk)).reshape(blk, N_HEADS * dk)
    gpow = jnp.exp(n_valid.astype(F32)[:, None, None] * log_gamma[None, :, None])
    gpow = jnp.broadcast_to(gpow, (pos.shape[0], N_HEADS, dv)).reshape(pos.shape[0], 1, N_HEADS * dv)
    return {
        "cos": jnp.concatenate([cos, cos], axis=-1),
        "sin": jnp.concatenate([-sin, sin], axis=-1),
        "decq": jnp.exp(dec),
        "deck": jnp.exp(-dec) * dk ** -0.5,
        "gpow": gpow,
    }


def kernel(x_prompt, x_sample, state_gla, state_ret, meta, ln_g, ln_b, w_ffn1_up, w_ffn1_down, w_in,
           w_alpha2, b_alpha, b_merge, gn_gla, gn_ret, w_o_gla, w_o_ret, w_out, w_ffn2_up, w_ffn2_down):
    nb, seq, d = x_prompt.shape
    nbs, seq_s, _ = x_sample.shape
    depth = ln_g.shape[0]
    n_meta = meta.shape[0]
    _, _, _, dk, dv = state_gla.shape
    hk, hv = N_HEADS * dk, N_HEADS * dv
    rank = w_alpha2.shape[1]
    alpha = (2.0 * depth) ** 0.25
    blk = min(MAIN_BLOCK, seq)
    sblk = SMALL_BLOCK
    nseq = MAIN_SEQS if nb % MAIN_SEQS == 0 else 1
    assert n_meta <= sblk and seq_s <= sblk

    hs = jnp.zeros((1 + nbs, sblk, d), F32)
    hs = hs.at[0, :n_meta].set(meta.astype(F32)).at[1:, :seq_s].set(x_sample)
    n_valid_s = jnp.asarray([n_meta] + [seq_s] * nbs, jnp.int32)
    valid_s = (jnp.arange(sblk)[None, :, None] < n_valid_s[:, None, None]).astype(F32)
    valid_s = jnp.broadcast_to(valid_s, (1 + nbs, sblk, dk))
    off = jnp.asarray([0] + [n_meta + PAST_LEN] * nbs, jnp.int32)
    tabs_s = _mix_tables(off[:, None] + jnp.arange(sblk)[None, :], sblk, n_valid_s, dk, dv)
    tabs_p = _mix_tables(n_meta + jnp.arange(seq)[None, :], blk, jnp.asarray([blk], jnp.int32), dk, dv)
    hp = x_prompt

    c_rg = 2 * hk + hv
    c_ag = c_rg + hv
    c_qr = c_ag + rank
    c_gr = c_qr + 2 * hk + hv
    c_mg = c_gr + hv
    w = {
        "mix": jnp.concatenate([w_in[:, :, :c_rg], w_in[:, :, c_qr:c_gr]], axis=2).astype(BF16),
        "ag": jnp.pad(w_in[:, :, c_ag:c_qr], ((0, 0), (0, 0), (0, GATE_RANK_PAD - rank))).astype(BF16),
        "a2": jnp.pad(w_alpha2, ((0, 0), (0, GATE_RANK_PAD - rank), (0, 0))).astype(BF16),
        "g": jnp.concatenate([w_in[:, :, c_rg:c_ag], w_in[:, :, c_gr:c_mg], w_in[:, :, c_mg:]],
                             axis=2).astype(BF16),
        "oa": w_o_gla.astype(BF16),
        "ob": w_o_ret.astype(BF16),
        "o": w_out.astype(BF16),
    }
    w1u, w1d = w_ffn1_up.astype(BF16), w_ffn1_down.astype(BF16)
    w2u, w2d = w_ffn2_up.astype(BF16), w_ffn2_down.astype(BF16)

    gla_p, ret_p, gla_s, ret_s = [], [], [], []
    for l in range(depth):
        vec = {
            "ba": b_alpha[l][None, :],
            "gna": gn_gla[l][None, :],
            "gnr": gn_ret[l][None, :],
            "bm": b_merge[l][None, :],
            "lng": ln_g[l, 1][None, :],
            "lnb": ln_b[l, 1][None, :],
        }

        def layer(h3, gla0, ret0, tabs, bsz, ns, valid):
            shape = h3.shape
            h = _ffn_ln(h3.reshape(-1, d), w1u, w1d, l, ln_g[l, 0][None, :], ln_b[l, 0][None, :], alpha)
            h, sg, sr = _attn(h.reshape(shape), gla0, ret0, w, l, vec, tabs, bsz, ns, alpha, valid)
            h = _ffn_ln(h.reshape(-1, d), w2u, w2d, l, ln_g[l, 2][None, :], ln_b[l, 2][None, :], alpha)
            return h.reshape(shape), sg, sr

        zero = jnp.zeros((1, N_HEADS, dk, dv), F32)
        hs, sg_s, sr_s = layer(hs, jnp.concatenate([zero, state_gla[l].astype(F32)], axis=0),
                               jnp.concatenate([zero, state_ret[l].astype(F32)], axis=0),
                               tabs_s, sblk, 1, valid_s)
        hp, sg_p, sr_p = layer(hp, sg_s[:1], sr_s[:1], tabs_p, blk, nseq, None)
        gla_p.append(sg_p.astype(x_prompt.dtype))
        ret_p.append(sr_p.astype(x_prompt.dtype))
        gla_s.append(sg_s[1:].astype(state_gla.dtype))
        ret_s.append(sr_s[1:].astype(state_ret.dtype))

    return (hp, hs[1:, :seq_s], jnp.stack(gla_p, axis=0), jnp.stack(ret_p, axis=0),
            jnp.stack(gla_s, axis=0), jnp.stack(ret_s, axis=0))
```

```python
import functools

import numpy as np
import jax
import jax.numpy as jnp
from jax import lax
from jax.experimental import pallas as pl
from jax.experimental.pallas import tpu as pltpu

F32 = jnp.float32
BF16 = jnp.bfloat16

N_HEADS = 4
GATE_RANK_PAD = 128
GATE_TAU = 16.0
PAST_LEN = 1024
ROPE_BASE = 10000.0
LN_EPS = 1e-5
GN_EPS = 1e-5
LOG2E = 1.4426950408889634

SUBLANES = 8
MXU_WIDTH = 256
MAIN_BLOCK = 256
MAIN_SEQS = 1
MAIN_SUBBLOCKS = 2
STAGGER = 26
DIAG_BLOCK = 128
DECAY_LEAD = 5
GATE_SLOTS = ((1, 1, 1, 1), 0, (1, 1, 1, 1), 0)
SMALL_BLOCK = 32
ROW_TILE = 1024
ROW_SUB = 256
FFN_CHUNK = 6 * MXU_WIDTH
VMEM_LIMIT = 56 * 1024 * 1024

_NT = (((1,), (1,)), ((), ()))
_TN = (((0,), (0,)), ((), ()))


def _dot(a, b):
    return jnp.dot(a, b, preferred_element_type=F32)


def _dot_nt(a, b):
    return lax.dot_general(a, b, _NT, preferred_element_type=F32)


def _dot_tn(a, b):
    return lax.dot_general(a, b, _TN, preferred_element_type=F32)


def _layer_norm(x, g, b):
    mu = jnp.mean(x, axis=-1, keepdims=True)
    xc = x - mu
    var = jnp.mean(xc * xc, axis=-1, keepdims=True)
    return xc * lax.rsqrt(var + LN_EPS) * g + b


def _sigmoid(x):
    return 0.5 + 0.5 * jnp.tanh(0.5 * x)


def _silu(x):
    hx = 0.5 * x
    return hx + hx * jnp.tanh(hx)


def _log_sigmoid(x):
    return jnp.minimum(x, 0.0) - jnp.log1p(jnp.exp(-jnp.abs(x)))


def _resident(shape):
    return pl.BlockSpec(shape, lambda *_: (0,) * len(shape), pipeline_mode=pl.Buffered(1))


def _layer_of(stacked):
    rest = stacked.shape[1:]
    return pl.BlockSpec((None,) + rest, lambda *a: (a[-1][0],) + (0,) * len(rest),
                        pipeline_mode=pl.Buffered(1))


def _row_tile(n_rows, limit):
    t = min(limit, n_rows)
    while n_rows % t or t % SUBLANES:
        t -= SUBLANES
    return t


def _ffn_ln_kernel(_layer_ref, h_ref, wup_ref, wdn_ref, g_ref, b_ref, o_ref, *, d_ff, cuts, sub, alpha):
    n_sub = h_ref.shape[0] // sub
    n_ch = len(cuts) - 1
    items = [(s, j) for s in range(n_sub) for j in range(n_ch)]
    hb = [None] * n_sub
    acc = [None] * n_sub

    def up(s, j):
        if hb[s] is None:
            hb[s] = h_ref[s * sub:(s + 1) * sub, :].astype(BF16)
        return (_dot(hb[s], wup_ref[:, cuts[j]:cuts[j + 1]]),
                _dot(hb[s], wup_ref[:, d_ff + cuts[j]:d_ff + cuts[j + 1]]))

    def finish(s):
        rows = slice(s * sub, (s + 1) * sub)
        o_ref[rows, :] = _layer_norm(alpha * h_ref[rows, :] + 0.5 * acc[s], g_ref[...], b_ref[...])

    nxt = up(*items[0])
    ready = None
    for idx, (s, j) in enumerate(items):
        a, b = nxt
        if idx + 1 < len(items):
            nxt = up(*items[idx + 1])
        if ready is not None:
            finish(ready)
            ready = None
        y = _dot((_silu(a) * b).astype(BF16), wdn_ref[cuts[j]:cuts[j + 1], :])
        acc[s] = y if acc[s] is None else acc[s] + y
        if j == n_ch - 1:
            ready = s
    finish(ready)


def _ffn_ln(h, w_up, w_down, layer, g, b, alpha):
    n, d = h.shape
    d_ff = w_down.shape[1]
    cuts = tuple(range(0, d_ff, FFN_CHUNK)) + (d_ff,)
    tm = _row_tile(n, ROW_TILE)
    sub = ROW_SUB if tm % ROW_SUB == 0 else tm
    return pl.pallas_call(
        functools.partial(_ffn_ln_kernel, d_ff=d_ff, cuts=cuts, sub=sub, alpha=alpha),
        grid_spec=pltpu.PrefetchScalarGridSpec(
            num_scalar_prefetch=1,
            grid=(n // tm,),
            in_specs=[
                pl.BlockSpec((tm, d), lambda i, l: (i, 0)),
                _layer_of(w_up),
                _layer_of(w_down),
                _resident(g.shape),
                _resident(b.shape),
            ],
            out_specs=pl.BlockSpec((tm, d), lambda i, l: (i, 0)),
        ),
        out_shape=jax.ShapeDtypeStruct((n, d), F32),
        compiler_params=pltpu.CompilerParams(
            dimension_semantics=("parallel",), vmem_limit_bytes=VMEM_LIMIT),
        name="ffn_ln",
    )(jnp.asarray([layer], jnp.int32), h, w_up, w_down, g, b)


def _head_norm(o, gain):
    mu = jnp.mean(o, axis=-1, keepdims=True)
    oc = o - mu
    var = jnp.mean(oc * oc, axis=-1, keepdims=True)
    return oc * lax.rsqrt(var + GN_EPS) * gain


def _attn_kernel(_layer_ref, *refs, nseq, nsub, stagger, blk, dblk, dk, dv, masked, alpha):
    n_in = 24 if masked else 23
    (h_ref, wmix_ref, wag_ref, wa2_ref, ba_ref, cos_ref, sin_ref, decq_ref, deck_ref, tri_ref,
     lvl_ref, gla0_ref, ret0_ref, gna_ref, gnr_ref, gpow_ref, wg_ref, bm_ref, woa_ref, wob_ref,
     wo_ref, lng_ref, lnb_ref) = refs[:23]
    valid_ref = refs[23] if masked else None
    o_ref, sg_ref, sr_ref, sgt_scr, sr_scr, b_scr, oa_scr, ob_scr = refs[n_in:]
    hk = N_HEADS * dk
    hv = N_HEADS * dv
    d = h_ref.shape[-1]
    n_diag = blk // dblk
    lv_diag = dblk.bit_length() - 1
    n_lv = blk.bit_length() - 1
    c0 = 2 * hk + hv
    pw = d // 2
    n_gate = 4 * d // pw
    j = pl.program_id(1)

    def shared(ref, s):
        return min(s, ref.shape[0] - 1)

    @pl.when(j == 0)
    def _():
        for s in range(nseq):
            for hd in range(N_HEADS):
                sgt_scr[s, hd] = gla0_ref[shared(gla0_ref, s), hd].T
            sr_scr[s] = ret0_ref[shared(ret0_ref, s)]

    lvl = lvl_ref[...]
    r8 = lax.broadcasted_iota(jnp.int32, (blk, hk), 0) & (SUBLANES - 1)
    causal = (lax.broadcasted_iota(jnp.int32, (blk, blk), 0)
              >= lax.broadcasted_iota(jnp.int32, (blk, blk), 1))

    def tile(x, i, hd):
        return x[i * dblk:(i + 1) * dblk, hd * dk:(hd + 1) * dk]

    loaded = {}

    def rows_of(s, u):
        if (s, u) not in loaded:
            h = h_ref[s, u * blk:(u + 1) * blk, :]
            loaded[(s, u)] = (h, h.astype(BF16))
        return loaded[(s, u)]

    decays = {}

    def decay(s, u):
        su = s * nsub + u
        _, hb = rows_of(s, u)
        out = decays[(s, u)] = {"fac": []}
        ag = _dot(hb, wag_ref[...])
        yield
        gate_in = _dot(ag.astype(BF16), wa2_ref[...])
        yield
        lg = _log_sigmoid(gate_in + ba_ref[...]) / GATE_TAU
        if masked:
            lg = lg * jnp.concatenate([valid_ref[s, u * blk:(u + 1) * blk, :]] * N_HEADS, axis=1)
        lgt = lg.T
        lg_hi = lgt.astype(BF16)
        rem = lgt - lg_hi.astype(F32)
        lg_mid = rem.astype(BF16)
        lg_lo = (rem - lg_mid.astype(F32)).astype(BF16)
        yield
        tri = tri_ref[...]
        b2 = ((_dot(lg_hi, tri) + _dot(lg_mid, tri) + _dot(lg_lo, tri)) * LOG2E).T
        b_scr[su] = b2
        out["b2"] = b2
        yield

        def in_group_rows(r):
            return jnp.concatenate(
                [jnp.broadcast_to(b_scr[su, pl.ds(SUBLANES * i + r, 1), :], (SUBLANES, hk))
                 for i in range(blk // SUBLANES)], axis=0)

        for lv in range(n_lv):
            if lv == 0:
                a = jnp.where(r8 < 2, in_group_rows(0),
                              jnp.where(r8 < 4, in_group_rows(2),
                                        jnp.where(r8 < 6, in_group_rows(4), in_group_rows(6))))
                diff = b2 - a
            elif lv == 1:
                diff = b2 - jnp.where(r8 < 4, in_group_rows(1), in_group_rows(5))
            elif lv == 2:
                diff = b2 - in_group_rows(3)
            else:
                grp = 2 << lv
                diff = jnp.concatenate(
                    [b2[g * grp:(g + 1) * grp] - b_scr[su, pl.ds(g * grp + grp // 2 - 1, 1), :]
                     for g in range(blk // grp)], axis=0)
            out["fac"].append(jnp.exp2(-jnp.abs(diff)))
            yield

    def block(s, u):
        rows = slice(u * blk, (u + 1) * blk)
        su = s * nsub + u
        h, hb = rows_of(s, u)
        gates = []

        def gate_pieces(n):
            for _ in range(n):
                p = len(gates)
                if p == n_gate:
                    return
                x = _dot(hb, wg_ref[:, p * pw:(p + 1) * pw])
                if p * pw < 2 * d:
                    gates.append(_silu(x))
                else:
                    gates.append(_sigmoid(x + bm_ref[:, p * pw - 2 * d:(p + 1) * pw - 2 * d]))

        def slot(k, i):
            n = GATE_SLOTS[k]
            return n if isinstance(n, int) else (n[i] if i < len(n) else 0)

        qg = _dot(hb, wmix_ref[:, 0:hk]) * (dk ** -0.5)
        kg = _dot(hb, wmix_ref[:, hk:2 * hk])
        if masked:
            valid = valid_ref[s, rows, :]
            kg = kg * jnp.concatenate([valid] * N_HEADS, axis=1)
        gate_pieces(slot(0, 0))
        yield
        vg = _dot(hb, wmix_ref[:, 2 * hk:2 * hk + hv]).astype(BF16)
        gate_pieces(slot(0, 1))
        yield
        qr = _dot(hb, wmix_ref[:, c0:c0 + hk])
        kr = _dot(hb, wmix_ref[:, c0 + hk:c0 + 2 * hk])
        gate_pieces(slot(0, 2))
        yield
        vr = _dot(hb, wmix_ref[:, c0 + 2 * hk:c0 + 2 * hk + hv]).astype(BF16)
        gate_pieces(slot(0, 3))
        yield
        b2 = decays[(s, u)]["b2"]
        fac = decays[(s, u)]["fac"]
        last = b_scr[su, pl.ds(blk - 1, 1), :]
        q_in = (qg * jnp.exp2(b2)).astype(BF16)
        k_out = (kg * jnp.exp2(-jnp.abs(last - b2))).astype(BF16)
        s_dec = jnp.exp2(last)
        qgb = qg.astype(BF16)
        kgb = kg.astype(BF16)

        def factors(lv):
            if lv < lv_diag:
                return (qg * fac[lv]).astype(BF16), (kg * fac[lv]).astype(BF16)
            return ((qg[dblk:] * fac[lv][dblk:]).astype(BF16), (kg[:dblk] * fac[lv][:dblk]).astype(BF16))

        nxt = factors(0)
        yield

        on_diag = lvl == lv_diag
        att = [[jnp.where(on_diag, _dot_nt(tile(qgb, i, hd), tile(kgb, i, hd)), 0.0)
                for hd in range(N_HEADS)] for i in range(n_diag)]
        yield
        for lv in range(lv_diag):
            qf, kf = nxt
            scores = [[_dot_nt(tile(qf, i, hd), tile(kf, i, hd)) for hd in range(N_HEADS)]
                      for i in range(n_diag)]
            if lv + 1 < n_lv:
                nxt = factors(lv + 1)
            at_level = lvl == lv
            att = [[jnp.where(at_level, scores[i][hd], att[i][hd]) for hd in range(N_HEADS)]
                   for i in range(n_diag)]
            gate_pieces(slot(1, lv))
            yield

        heads = [(slice(hd * dk, (hd + 1) * dk), slice(hd * dv, (hd + 1) * dv)) for hd in range(N_HEADS)]
        if n_diag == 2:
            qf, kf = nxt
            below = [_dot_nt(qf[:, ks], kf[:, ks]) for ks, _ in heads]
            att_b = [(att[0][hd].astype(BF16),
                      jnp.concatenate([below[hd], att[1][hd]], axis=1).astype(BF16))
                     for hd in range(N_HEADS)]
        else:
            att_b = [(att[0][hd].astype(BF16),) for hd in range(N_HEADS)]
        st = [sgt_scr[s, hd] for hd in range(N_HEADS)]
        st_b = [x.astype(BF16) for x in st]
        yield
        gna = gna_ref[...]
        for hd, (ks, vs) in enumerate(heads):
            v = vg[:, vs]
            if n_diag == 2:
                o = jnp.concatenate([_dot(att_b[hd][0], v[:dblk]), _dot(att_b[hd][1], v)], axis=0)
            else:
                o = _dot(att_b[hd][0], v)
            o = o + _dot_nt(q_in[:, ks], st_b[hd])
            sgt_scr[s, hd] = st[hd] * s_dec[:, ks] + _dot_tn(v, k_out[:, ks])
            oa_scr[su, :, vs] = _head_norm(o, gna[:, vs])
            gate_pieces(slot(2, hd))
            yield

        cos = cos_ref[shared(cos_ref, s), rows, :]
        sin = sin_ref[shared(sin_ref, s), rows, :]
        decq = decq_ref[...]
        deck = deck_ref[...]
        gpow = gpow_ref[shared(gpow_ref, s)]
        gnr = gnr_ref[...]
        qb, kb = [], []
        for ks, _ in heads:
            q = qr[:, ks]
            k = kr[:, ks]
            q = (q * cos + pltpu.roll(q, dk // 2, 1) * sin) * decq[:, ks]
            k = (k * cos + pltpu.roll(k, dk // 2, 1) * sin) * deck[:, ks]
            if masked:
                k = k * valid
            qb.append(q.astype(BF16))
            kb.append(k.astype(BF16))
        s0 = [sr_scr[s, hd] for hd in range(N_HEADS)]
        s0_b = [x.astype(BF16) for x in s0]
        yield
        sc_b = [None] * N_HEADS

        def finish_head(hd):
            _, vs = heads[hd]
            v = vr[:, vs]
            o = _dot(sc_b[hd], v) + _dot(qb[hd], s0_b[hd])
            sr_scr[s, hd] = (s0[hd] + _dot_tn(kb[hd], v)) * gpow[:, vs]
            ob_scr[su, :, vs] = _head_norm(o, gnr[:, vs])

        for hd in range(N_HEADS):
            raw = _dot_nt(qb[hd], kb[hd])
            if hd > 0:
                finish_head(hd - 1)
            sc_b[hd] = jnp.where(causal, raw, 0.0).astype(BF16)
            gate_pieces(slot(3, hd))
            yield
        finish_head(N_HEADS - 1)
        gate_pieces(n_gate)
        rg, gr, ga, gb = [jnp.concatenate(gates[i * (d // pw):(i + 1) * (d // pw)], axis=1)
                          for i in range(4)]
        lhs_a = (rg * oa_scr[su]).astype(BF16)
        yield
        ya = _dot(lhs_a, woa_ref[...])
        lhs_b = (gr * ob_scr[su]).astype(BF16)
        yield
        yb = _dot(lhs_b, wob_ref[...])
        yield
        y = _dot((ga * ya + gb * yb).astype(BF16), wo_ref[...])
        o_ref[s, rows, :] = _layer_norm(alpha * h + y, lng_ref[...], lnb_ref[...])
        yield

    units = [(s, u) for u in range(nsub) for s in range(nseq)]
    starts = sorted([(DECAY_LEAD * k, 0, decay(*su)) for k, su in enumerate(units)]
                    + [(stagger * k, 1, block(*su)) for k, su in enumerate(units)], key=lambda e: e[:2])
    running = []
    tick = 0
    while starts or running:
        while starts and starts[0][0] <= tick:
            running.append(starts.pop(0)[2])
        for g in list(running):
            try:
                next(g)
            except StopIteration:
                running.remove(g)
        tick += 1

    @pl.when(j == pl.num_programs(1) - 1)
    def _():
        for s in range(nseq):
            for hd in range(N_HEADS):
                sg_ref[s, hd] = sgt_scr[s, hd].T
            sr_ref[s] = sr_scr[s]


def _level_matrix(n):
    r = np.arange(n)[:, None]
    c = np.arange(n)[None, :]
    x = np.maximum(r ^ c, 1)
    lvl = np.where(r > c, np.floor(np.log2(x)), np.where(r == c, n.bit_length() - 1, -1))
    return lvl.astype(np.int32)


def _attn(h, gla0, ret0, w, layer, vec, tabs, blk, nseq, alpha, valid=None):
    nb, t, d = h.shape
    _, _, dk, dv = gla0.shape
    hk, hv = N_HEADS * dk, N_HEADS * dv
    dblk = min(blk, DIAG_BLOCK)
    nsub = MAIN_SUBBLOCKS if (valid is None and t % (MAIN_SUBBLOCKS * blk) == 0) else 1
    tstep = nsub * blk
    assert t % tstep == 0 and blk & (blk - 1) == 0 and SUBLANES <= blk <= 256 and blk // dblk <= 2
    assert nb % nseq == 0
    lvl = jnp.asarray(_level_matrix(dblk))
    tri = jnp.asarray(np.triu(np.ones((blk, blk), np.float32)), BF16)
    masked = valid is not None

    def per_seq(a, width):
        if a.shape[0] > 1:
            return pl.BlockSpec((nseq, tstep, width), lambda b, j, l: (b, j, 0))
        return pl.BlockSpec((1, tstep, width), lambda b, j, l: (0, j, 0))

    def state_in(a):
        if a.shape[0] > 1:
            return pl.BlockSpec((nseq, N_HEADS, dk, dv), lambda b, j, l: (b, 0, 0, 0))
        return pl.BlockSpec((1, N_HEADS, dk, dv), lambda b, j, l: (0, 0, 0, 0))

    seq = pl.BlockSpec((nseq, tstep, d), lambda b, j, l: (b, j, 0))
    args = [h, w["mix"], w["ag"], w["a2"], vec["ba"], tabs["cos"], tabs["sin"], tabs["decq"], tabs["deck"],
            tri, lvl, gla0, ret0, vec["gna"], vec["gnr"], tabs["gpow"], w["g"], vec["bm"], w["oa"], w["ob"],
            w["o"], vec["lng"], vec["lnb"]]
    in_specs = [
        seq,
        _layer_of(w["mix"]), _layer_of(w["ag"]), _layer_of(w["a2"]),
        _resident(vec["ba"].shape),
        per_seq(tabs["cos"], dk), per_seq(tabs["sin"], dk),
        _resident(tabs["decq"].shape), _resident(tabs["deck"].shape),
        _resident(tri.shape), _resident(lvl.shape),
        state_in(gla0), state_in(ret0),
        _resident(vec["gna"].shape), _resident(vec["gnr"].shape),
        (pl.BlockSpec((nseq, 1, hv), lambda b, j, l: (b, 0, 0)) if tabs["gpow"].shape[0] > 1
         else pl.BlockSpec((1, 1, hv), lambda b, j, l: (0, 0, 0))),
        _layer_of(w["g"]), _resident(vec["bm"].shape), _layer_of(w["oa"]),
        _layer_of(w["ob"]), _layer_of(w["o"]), _resident(vec["lng"].shape),
        _resident(vec["lnb"].shape),
    ]
    if masked:
        in_specs.append(pl.BlockSpec((nseq, tstep, dk), lambda b, j, l: (b, j, 0)))
        args.append(valid)
    state_out = pl.BlockSpec((nseq, N_HEADS, dk, dv), lambda b, j, l: (b, 0, 0, 0))
    return pl.pallas_call(
        functools.partial(_attn_kernel, nseq=nseq, nsub=nsub, stagger=STAGGER, blk=blk, dblk=dblk, dk=dk, dv=dv,
                          masked=masked, alpha=alpha),
        grid_spec=pltpu.PrefetchScalarGridSpec(
            num_scalar_prefetch=1,
            grid=(nb // nseq, t // tstep),
            in_specs=in_specs,
            out_specs=[seq, state_out, state_out],
            scratch_shapes=[
                pltpu.VMEM((nseq, N_HEADS, dv, dk), F32), pltpu.VMEM((nseq, N_HEADS, dk, dv), F32),
                pltpu.VMEM((nseq * nsub, blk, hk), F32), pltpu.VMEM((nseq * nsub, blk, hv), F32),
                pltpu.VMEM((nseq * nsub, blk, hv), F32)],
        ),
        out_shape=[jax.ShapeDtypeStruct((nb, t, d), F32),
                   jax.ShapeDtypeStruct((nb, N_HEADS, dk, dv), F32),
                   jax.ShapeDtypeStruct((nb, N_HEADS, dk, dv), F32)],
        compiler_params=pltpu.CompilerParams(
            dimension_semantics=("parallel", "arbitrary"), vmem_limit_bytes=VMEM_LIMIT),
        name="attn",
    )(jnp.asarray([layer], jnp.int32), *args)


def _mix_tables(pos, blk, n_valid, dk, dv):
    half = dk // 2
    inv = ROPE_BASE ** (-jnp.arange(half, dtype=F32) / half)
    ang = pos.astype(F32)[..., None] * inv
    cos = jnp.cos(ang)
    sin = jnp.sin(ang)
    log_gamma = jnp.log1p(-(2.0 ** (-5.0 - jnp.arange(N_HEADS, dtype=F32))))
    step = jnp.arange(blk, dtype=F32)[:, None, None] + 1.0
    dec = jnp.broadcast_to(step * log_gamma[None, :, None], (blk, N_HEADS, dk)).reshape(blk, N_HEADS * dk)
    gpow = jnp.exp(n_valid.astype(F32)[:, None, None] * log_gamma[None, :, None])
    gpow = jnp.broadcast_to(gpow, (pos.shape[0], N_HEADS, dv)).reshape(pos.shape[0], 1, N_HEADS * dv)
    return {
        "cos": jnp.concatenate([cos, cos], axis=-1),
        "sin": jnp.concatenate([-sin, sin], axis=-1),
        "decq": jnp.exp(dec),
        "deck": jnp.exp(-dec) * dk ** -0.5,
        "gpow": gpow,
    }


def kernel(x_prompt, x_sample, state_gla, state_ret, meta, ln_g, ln_b, w_ffn1_up, w_ffn1_down, w_in,
           w_alpha2, b_alpha, b_merge, gn_gla, gn_ret, w_o_gla, w_o_ret, w_out, w_ffn2_up, w_ffn2_down):
    nb, seq, d = x_prompt.shape
    nbs, seq_s, _ = x_sample.shape
    depth = ln_g.shape[0]
    n_meta = meta.shape[0]
    _, _, _, dk, dv = state_gla.shape
    hk, hv = N_HEADS * dk, N_HEADS * dv
    rank = w_alpha2.shape[1]
    alpha = (2.0 * depth) ** 0.25
    blk = min(MAIN_BLOCK, seq)
    sblk = SMALL_BLOCK
    nseq = MAIN_SEQS if nb % MAIN_SEQS == 0 else 1
    assert n_meta <= sblk and seq_s <= sblk

    hs = jnp.zeros((1 + nbs, sblk, d), F32)
    hs = hs.at[0, :n_meta].set(meta.astype(F32)).at[1:, :seq_s].set(x_sample)
    n_valid_s = jnp.asarray([n_meta] + [seq_s] * nbs, jnp.int32)
    valid_s = (jnp.arange(sblk)[None, :, None] < n_valid_s[:, None, None]).astype(F32)
    valid_s = jnp.broadcast_to(valid_s, (1 + nbs, sblk, dk))
    off = jnp.asarray([0] + [n_meta + PAST_LEN] * nbs, jnp.int32)
    tabs_s = _mix_tables(off[:, None] + jnp.arange(sblk)[None, :], sblk, n_valid_s, dk, dv)
    tabs_p = _mix_tables(n_meta + jnp.arange(seq)[None, :], blk, jnp.asarray([blk], jnp.int32), dk, dv)
    hp = x_prompt

    c_rg = 2 * hk + hv
    c_ag = c_rg + hv
    c_qr = c_ag + rank
    c_gr = c_qr + 2 * hk + hv
    c_mg = c_gr + hv
    w = {
        "mix": jnp.concatenate([w_in[:, :, :c_rg], w_in[:, :, c_qr:c_gr]], axis=2).astype(BF16),
        "ag": jnp.pad(w_in[:, :, c_ag:c_qr], ((0, 0), (0, 0), (0, GATE_RANK_PAD - rank))).astype(BF16),
        "a2": jnp.pad(w_alpha2, ((0, 0), (0, GATE_RANK_PAD - rank), (0, 0))).astype(BF16),
        "g": jnp.concatenate([w_in[:, :, c_rg:c_ag], w_in[:, :, c_gr:c_mg], w_in[:, :, c_mg:]],
                             axis=2).astype(BF16),
        "oa": w_o_gla.astype(BF16),
        "ob": w_o_ret.astype(BF16),
        "o": w_out.astype(BF16),
    }
    w1u, w1d = w_ffn1_up.astype(BF16), w_ffn1_down.astype(BF16)
    w2u, w2d = w_ffn2_up.astype(BF16), w_ffn2_down.astype(BF16)

    gla_p, ret_p, gla_s, ret_s = [], [], [], []
    for l in range(depth):
        vec = {
            "ba": b_alpha[l][None, :],
            "gna": gn_gla[l][None, :],
            "gnr": gn_ret[l][None, :],
            "bm": b_merge[l][None, :],
            "lng": ln_g[l, 1][None, :],
            "lnb": ln_b[l, 1][None, :],
        }

        def layer(h3, gla0, ret0, tabs, bsz, ns, valid):
            shape = h3.shape
            h = _ffn_ln(h3.reshape(-1, d), w1u, w1d, l, ln_g[l, 0][None, :], ln_b[l, 0][None, :], alpha)
            h, sg, sr = _attn(h.reshape(shape), gla0, ret0, w, l, vec, tabs, bsz, ns, alpha, valid)
            h = _ffn_ln(h.reshape(-1, d), w2u, w2d, l, ln_g[l, 2][None, :], ln_b[l, 2][None, :], alpha)
            return h.reshape(shape), sg, sr

        zero = jnp.zeros((1, N_HEADS, dk, dv), F32)
        hs, sg_s, sr_s = layer(hs, jnp.concatenate([zero, state_gla[l].astype(F32)], axis=0),
                               jnp.concatenate([zero, state_ret[l].astype(F32)], axis=0),
                               tabs_s, sblk, 1, valid_s)
        hp, sg_p, sr_p = layer(hp, sg_s[:1], sr_s[:1], tabs_p, blk, nseq, None)
        gla_p.append(sg_p.astype(x_prompt.dtype))
        ret_p.append(sr_p.astype(x_prompt.dtype))
        gla_s.append(sg_s[1:].astype(state_gla.dtype))
        ret_s.append(sr_s[1:].astype(state_ret.dtype))

    return (hp, hs[1:, :seq_s], jnp.stack(gla_p, axis=0), jnp.stack(ret_p, axis=0),
            jnp.stack(gla_s, axis=0), jnp.stack(ret_s, axis=0))
```

```python
import functools

import numpy as np
import jax
import jax.numpy as jnp
from jax import lax
from jax.experimental import pallas as pl
from jax.experimental.pallas import tpu as pltpu

F32 = jnp.float32
BF16 = jnp.bfloat16

N_HEADS = 4
GATE_RANK_PAD = 128
GATE_TAU = 16.0
PAST_LEN = 1024
ROPE_BASE = 10000.0
LN_EPS = 1e-5
GN_EPS = 1e-5
LOG2E = 1.4426950408889634

SUBLANES = 8
MXU_WIDTH = 256
MAIN_BLOCK = 256
MAIN_SEQS = 1
MAIN_SUBBLOCKS = 2
STAGGER = 26
DIAG_BLOCK = 128
GATE_SLOTS = (1, 1, (1, 0, 1, 0, 1, 0, 1, 0), 0, (1, 1, 0, 0), 0)
SMALL_BLOCK = 32
ROW_TILE = 1024
ROW_SUB = 256
FFN_CHUNK = 4 * MXU_WIDTH
VMEM_LIMIT = 56 * 1024 * 1024

_NT = (((1,), (1,)), ((), ()))
_TN = (((0,), (0,)), ((), ()))


def _dot(a, b):
    return jnp.dot(a, b, preferred_element_type=F32)


def _dot_nt(a, b):
    return lax.dot_general(a, b, _NT, preferred_element_type=F32)


def _dot_tn(a, b):
    return lax.dot_general(a, b, _TN, preferred_element_type=F32)


def _layer_norm(x, g, b):
    mu = jnp.mean(x, axis=-1, keepdims=True)
    xc = x - mu
    var = jnp.mean(xc * xc, axis=-1, keepdims=True)
    return xc * lax.rsqrt(var + LN_EPS) * g + b


def _sigmoid(x):
    return 0.5 + 0.5 * jnp.tanh(0.5 * x)


def _silu(x):
    hx = 0.5 * x
    return hx + hx * jnp.tanh(hx)


def _log_sigmoid(x):
    return jnp.minimum(x, 0.0) - jnp.log1p(jnp.exp(-jnp.abs(x)))


def _resident(shape):
    return pl.BlockSpec(shape, lambda *_: (0,) * len(shape), pipeline_mode=pl.Buffered(1))


def _layer_of(stacked):
    rest = stacked.shape[1:]
    return pl.BlockSpec((None,) + rest, lambda *a: (a[-1][0],) + (0,) * len(rest),
                        pipeline_mode=pl.Buffered(1))


def _row_tile(n_rows, limit):
    t = min(limit, n_rows)
    while n_rows % t or t % SUBLANES:
        t -= SUBLANES
    return t


def _ffn_ln_kernel(_layer_ref, h_ref, wup_ref, wdn_ref, g_ref, b_ref, o_ref, *, d_ff, cuts, sub, alpha):
    n_sub = h_ref.shape[0] // sub
    n_ch = len(cuts) - 1
    items = [(s, j) for s in range(n_sub) for j in range(n_ch)]
    hb = [None] * n_sub
    acc = [None] * n_sub

    def up(s, j):
        if hb[s] is None:
            hb[s] = h_ref[s * sub:(s + 1) * sub, :].astype(BF16)
        return (_dot(hb[s], wup_ref[:, cuts[j]:cuts[j + 1]]),
                _dot(hb[s], wup_ref[:, d_ff + cuts[j]:d_ff + cuts[j + 1]]))

    def finish(s):
        rows = slice(s * sub, (s + 1) * sub)
        o_ref[rows, :] = _layer_norm(alpha * h_ref[rows, :] + 0.5 * acc[s], g_ref[...], b_ref[...])

    nxt = up(*items[0])
    ready = None
    for idx, (s, j) in enumerate(items):
        a, b = nxt
        if idx + 1 < len(items):
            nxt = up(*items[idx + 1])
        if ready is not None:
            finish(ready)
            ready = None
        y = _dot((_silu(a) * b).astype(BF16), wdn_ref[cuts[j]:cuts[j + 1], :])
        acc[s] = y if acc[s] is None else acc[s] + y
        if j == n_ch - 1:
            ready = s
    finish(ready)


def _ffn_ln(h, w_up, w_down, layer, g, b, alpha):
    n, d = h.shape
    d_ff = w_down.shape[1]
    cuts = tuple(range(0, d_ff, FFN_CHUNK)) + (d_ff,)
    tm = _row_tile(n, ROW_TILE)
    sub = ROW_SUB if tm % ROW_SUB == 0 else tm
    return pl.pallas_call(
        functools.partial(_ffn_ln_kernel, d_ff=d_ff, cuts=cuts, sub=sub, alpha=alpha),
        grid_spec=pltpu.PrefetchScalarGridSpec(
            num_scalar_prefetch=1,
            grid=(n // tm,),
            in_specs=[
                pl.BlockSpec((tm, d), lambda i, l: (i, 0)),
                _layer_of(w_up),
                _layer_of(w_down),
                _resident(g.shape),
                _resident(b.shape),
            ],
            out_specs=pl.BlockSpec((tm, d), lambda i, l: (i, 0)),
        ),
        out_shape=jax.ShapeDtypeStruct((n, d), F32),
        compiler_params=pltpu.CompilerParams(
            dimension_semantics=("parallel",), vmem_limit_bytes=VMEM_LIMIT),
        name="ffn_ln",
    )(jnp.asarray([layer], jnp.int32), h, w_up, w_down, g, b)


def _head_norm(o, gain):
    mu = jnp.mean(o, axis=-1, keepdims=True)
    oc = o - mu
    var = jnp.mean(oc * oc, axis=-1, keepdims=True)
    return oc * lax.rsqrt(var + GN_EPS) * gain


def _attn_kernel(_layer_ref, *refs, nseq, nsub, stagger, blk, dblk, dk, dv, masked, alpha):
    n_in = 24 if masked else 23
    (h_ref, wmix_ref, wag_ref, wa2_ref, ba_ref, cos_ref, sin_ref, decq_ref, deck_ref, tri_ref,
     lvl_ref, gla0_ref, ret0_ref, gna_ref, gnr_ref, gpow_ref, wg_ref, bm_ref, woa_ref, wob_ref,
     wo_ref, lng_ref, lnb_ref) = refs[:23]
    valid_ref = refs[23] if masked else None
    o_ref, sg_ref, sr_ref, sgt_scr, sr_scr, b_scr, oa_scr, ob_scr = refs[n_in:]
    hk = N_HEADS * dk
    hv = N_HEADS * dv
    d = h_ref.shape[-1]
    n_diag = blk // dblk
    lv_diag = dblk.bit_length() - 1
    n_lv = blk.bit_length() - 1
    c0 = 2 * hk + hv
    pw = d // 2
    n_gate = 4 * d // pw
    j = pl.program_id(1)

    def shared(ref, s):
        return min(s, ref.shape[0] - 1)

    @pl.when(j == 0)
    def _():
        for s in range(nseq):
            for hd in range(N_HEADS):
                sgt_scr[s, hd] = gla0_ref[shared(gla0_ref, s), hd].T
            sr_scr[s] = ret0_ref[shared(ret0_ref, s)]

    lvl = lvl_ref[...]
    r8 = lax.broadcasted_iota(jnp.int32, (blk, hk), 0) & (SUBLANES - 1)
    causal = (lax.broadcasted_iota(jnp.int32, (blk, blk), 0)
              >= lax.broadcasted_iota(jnp.int32, (blk, blk), 1))

    def tile(x, i, hd):
        return x[i * dblk:(i + 1) * dblk, hd * dk:(hd + 1) * dk]

    def block(s, u):
        rows = slice(u * blk, (u + 1) * blk)
        su = s * nsub + u
        h = h_ref[s, rows, :]
        hb = h.astype(BF16)
        gates = []

        def gate_pieces(n):
            for _ in range(n):
                p = len(gates)
                if p == n_gate:
                    return
                x = _dot(hb, wg_ref[:, p * pw:(p + 1) * pw])
                if p * pw < 2 * d:
                    gates.append(_silu(x))
                else:
                    gates.append(_sigmoid(x + bm_ref[:, p * pw - 2 * d:(p + 1) * pw - 2 * d]))

        def slot(k, i):
            n = GATE_SLOTS[k]
            return n if isinstance(n, int) else (n[i] if i < len(n) else 0)

        ag = _dot(hb, wag_ref[...])
        qg = _dot(hb, wmix_ref[:, 0:hk]) * (dk ** -0.5)
        yield
        gate_in = _dot(ag.astype(BF16), wa2_ref[...])
        kg = _dot(hb, wmix_ref[:, hk:2 * hk])
        vg = _dot(hb, wmix_ref[:, 2 * hk:2 * hk + hv]).astype(BF16)
        gate_pieces(slot(0, 0))
        lg = _log_sigmoid(gate_in + ba_ref[...]) / GATE_TAU
        if masked:
            valid = valid_ref[s, rows, :]
            vmask = jnp.concatenate([valid] * N_HEADS, axis=1)
            lg = lg * vmask
            kg = kg * vmask
        yield
        qr = _dot(hb, wmix_ref[:, c0:c0 + hk])
        kr = _dot(hb, wmix_ref[:, c0 + hk:c0 + 2 * hk])
        lg_hi = lg.astype(BF16)
        rem = lg - lg_hi.astype(F32)
        lg_mid = rem.astype(BF16)
        lg_lo = (rem - lg_mid.astype(F32)).astype(BF16)
        tri = tri_ref[...]
        b2 = (_dot(tri, lg_hi) + _dot(tri, lg_mid) + _dot(tri, lg_lo)) * LOG2E
        b_scr[su] = b2
        vr = _dot(hb, wmix_ref[:, c0 + 2 * hk:c0 + 2 * hk + hv]).astype(BF16)
        gate_pieces(slot(1, 0))
        yield

        def in_group_rows(r):
            return jnp.concatenate(
                [jnp.broadcast_to(b_scr[su, pl.ds(SUBLANES * i + r, 1), :], (SUBLANES, hk))
                 for i in range(blk // SUBLANES)], axis=0)

        def level_factor(lv):
            if lv == 0:
                a = jnp.where(r8 < 2, in_group_rows(0),
                              jnp.where(r8 < 4, in_group_rows(2),
                                        jnp.where(r8 < 6, in_group_rows(4), in_group_rows(6))))
                diff = b2 - a
            elif lv == 1:
                diff = b2 - jnp.where(r8 < 4, in_group_rows(1), in_group_rows(5))
            elif lv == 2:
                diff = b2 - in_group_rows(3)
            else:
                grp = 2 << lv
                diff = jnp.concatenate(
                    [b2[g * grp:(g + 1) * grp] - b_scr[su, pl.ds(g * grp + grp // 2 - 1, 1), :]
                     for g in range(blk // grp)], axis=0)
            return jnp.exp2(-jnp.abs(diff))

        fac = []
        for lv in range(n_lv):
            gate_pieces(slot(2, lv))
            fac.append(level_factor(lv))
            yield
        last = b_scr[su, pl.ds(blk - 1, 1), :]
        q_in = (qg * jnp.exp2(b2)).astype(BF16)
        k_out = (kg * jnp.exp2(-jnp.abs(last - b2))).astype(BF16)
        s_dec = jnp.exp2(last)
        yield

        qgb = qg.astype(BF16)
        kgb = kg.astype(BF16)
        on_diag = lvl == lv_diag
        att = [[jnp.where(on_diag, _dot_nt(tile(qgb, i, hd), tile(kgb, i, hd)), 0.0)
                for hd in range(N_HEADS)] for i in range(n_diag)]
        yield
        for lv in range(lv_diag):
            qf = (qg * fac[lv]).astype(BF16)
            kf = (kg * fac[lv]).astype(BF16)
            at_level = lvl == lv
            for i in range(n_diag):
                for hd in range(N_HEADS):
                    att[i][hd] = jnp.where(at_level, _dot_nt(tile(qf, i, hd), tile(kf, i, hd)), att[i][hd])
            gate_pieces(slot(3, lv))
            yield
        if n_diag == 2:
            qf = (qg[dblk:] * fac[lv_diag][dblk:]).astype(BF16)
            kf = (kg[:dblk] * fac[lv_diag][:dblk]).astype(BF16)
        gna = gna_ref[...]
        for hd in range(N_HEADS):
            ks = slice(hd * dk, (hd + 1) * dk)
            vs = slice(hd * dv, (hd + 1) * dv)
            v = vg[:, vs]
            if n_diag == 2:
                below = _dot_nt(qf[:, ks], kf[:, ks])
                o = jnp.concatenate([
                    _dot(att[0][hd].astype(BF16), v[:dblk]),
                    _dot(jnp.concatenate([below, att[1][hd]], axis=1).astype(BF16), v)], axis=0)
            else:
                o = _dot(att[0][hd].astype(BF16), v)
            o = o + _dot_nt(q_in[:, ks], sgt_scr[s, hd].astype(BF16))
            oa_scr[su, :, vs] = _head_norm(o, gna[:, vs])
            gate_pieces(slot(4, hd))
            yield

        cos = cos_ref[shared(cos_ref, s), rows, :]
        sin = sin_ref[shared(sin_ref, s), rows, :]
        decq = decq_ref[...]
        deck = deck_ref[...]
        gpow = gpow_ref[shared(gpow_ref, s)]
        gnr = gnr_ref[...]
        ret_k = []
        for hd in range(N_HEADS):
            ks = slice(hd * dk, (hd + 1) * dk)
            vs = slice(hd * dv, (hd + 1) * dv)
            q = qr[:, ks]
            k = kr[:, ks]
            q = (q * cos + pltpu.roll(q, dk // 2, 1) * sin) * decq[:, ks]
            k = (k * cos + pltpu.roll(k, dk // 2, 1) * sin) * deck[:, ks]
            if masked:
                k = k * valid
            qb = q.astype(BF16)
            kb = k.astype(BF16)
            v = vr[:, vs]
            sc = jnp.where(causal, _dot_nt(qb, kb), 0.0)
            o = _dot(sc.astype(BF16), v) + _dot(qb, sr_scr[s, hd].astype(BF16))
            ret_k.append(kb)
            ob_scr[su, :, vs] = _head_norm(o, gnr[:, vs])
            gate_pieces(slot(5, hd))
            yield

        gate_pieces(n_gate)
        rg, gr, ga, gb = [jnp.concatenate(gates[i * (d // pw):(i + 1) * (d // pw)], axis=1)
                          for i in range(4)]
        ya = _dot((rg * oa_scr[su]).astype(BF16), woa_ref[...])
        yb = _dot((gr * ob_scr[su]).astype(BF16), wob_ref[...])
        y = _dot((ga * ya + gb * yb).astype(BF16), wo_ref[...])
        for hd in range(N_HEADS):
            ks = slice(hd * dk, (hd + 1) * dk)
            vs = slice(hd * dv, (hd + 1) * dv)
            sgt_scr[s, hd] = sgt_scr[s, hd] * s_dec[:, ks] + _dot_tn(vg[:, vs], k_out[:, ks])
            sr_scr[s, hd] = (sr_scr[s, hd] + _dot_tn(ret_k[hd], vr[:, vs])) * gpow[:, vs]
        o_ref[s, rows, :] = _layer_norm(alpha * h + y, lng_ref[...], lnb_ref[...])
        yield

    waiting = [block(s, u) for u in range(nsub) for s in range(nseq)]
    running = []
    tick = 0
    while waiting or running:
        if waiting and tick % stagger == 0:
            running.append(waiting.pop(0))
        for g in list(running):
            try:
                next(g)
            except StopIteration:
                running.remove(g)
        tick += 1

    @pl.when(j == pl.num_programs(1) - 1)
    def _():
        for s in range(nseq):
            for hd in range(N_HEADS):
                sg_ref[s, hd] = sgt_scr[s, hd].T
            sr_ref[s] = sr_scr[s]


def _level_matrix(n):
    r = np.arange(n)[:, None]
    c = np.arange(n)[None, :]
    x = np.maximum(r ^ c, 1)
    lvl = np.where(r > c, np.floor(np.log2(x)), np.where(r == c, n.bit_length() - 1, -1))
    return lvl.astype(np.int32)


def _attn(h, gla0, ret0, w, layer, vec, tabs, blk, nseq, alpha, valid=None):
    nb, t, d = h.shape
    _, _, dk, dv = gla0.shape
    hk, hv = N_HEADS * dk, N_HEADS * dv
    dblk = min(blk, DIAG_BLOCK)
    nsub = MAIN_SUBBLOCKS if (valid is None and t % (MAIN_SUBBLOCKS * blk) == 0) else 1
    tstep = nsub * blk
    assert t % tstep == 0 and blk & (blk - 1) == 0 and SUBLANES <= blk <= 256 and blk // dblk <= 2
    assert nb % nseq == 0
    lvl = jnp.asarray(_level_matrix(dblk))
    tri = jnp.asarray(np.tril(np.ones((blk, blk), np.float32)), BF16)
    masked = valid is not None

    def per_seq(a, width):
        if a.shape[0] > 1:
            return pl.BlockSpec((nseq, tstep, width), lambda b, j, l: (b, j, 0))
        return pl.BlockSpec((1, tstep, width), lambda b, j, l: (0, j, 0))

    def state_in(a):
        if a.shape[0] > 1:
            return pl.BlockSpec((nseq, N_HEADS, dk, dv), lambda b, j, l: (b, 0, 0, 0))
        return pl.BlockSpec((1, N_HEADS, dk, dv), lambda b, j, l: (0, 0, 0, 0))

    seq = pl.BlockSpec((nseq, tstep, d), lambda b, j, l: (b, j, 0))
    args = [h, w["mix"], w["ag"], w["a2"], vec["ba"], tabs["cos"], tabs["sin"], tabs["decq"], tabs["deck"],
            tri, lvl, gla0, ret0, vec["gna"], vec["gnr"], tabs["gpow"], w["g"], vec["bm"], w["oa"], w["ob"],
            w["o"], vec["lng"], vec["lnb"]]
    in_specs = [
        seq,
        _layer_of(w["mix"]), _layer_of(w["ag"]), _layer_of(w["a2"]),
        _resident(vec["ba"].shape),
        per_seq(tabs["cos"], dk), per_seq(tabs["sin"], dk),
        _resident(tabs["decq"].shape), _resident(tabs["deck"].shape),
        _resident(tri.shape), _resident(lvl.shape),
        state_in(gla0), state_in(ret0),
        _resident(vec["gna"].shape), _resident(vec["gnr"].shape),
        (pl.BlockSpec((nseq, 1, hv), lambda b, j, l: (b, 0, 0)) if tabs["gpow"].shape[0] > 1
         else pl.BlockSpec((1, 1, hv), lambda b, j, l: (0, 0, 0))),
        _layer_of(w["g"]), _resident(vec["bm"].shape), _layer_of(w["oa"]),
        _layer_of(w["ob"]), _layer_of(w["o"]), _resident(vec["lng"].shape),
        _resident(vec["lnb"].shape),
    ]
    if masked:
        in_specs.append(pl.BlockSpec((nseq, tstep, dk), lambda b, j, l: (b, j, 0)))
        args.append(valid)
    state_out = pl.BlockSpec((nseq, N_HEADS, dk, dv), lambda b, j, l: (b, 0, 0, 0))
    return pl.pallas_call(
        functools.partial(_attn_kernel, nseq=nseq, nsub=nsub, stagger=STAGGER, blk=blk, dblk=dblk, dk=dk, dv=dv,
                          masked=masked, alpha=alpha),
        grid_spec=pltpu.PrefetchScalarGridSpec(
            num_scalar_prefetch=1,
            grid=(nb // nseq, t // tstep),
            in_specs=in_specs,
            out_specs=[seq, state_out, state_out],
            scratch_shapes=[
                pltpu.VMEM((nseq, N_HEADS, dv, dk), F32), pltpu.VMEM((nseq, N_HEADS, dk, dv), F32),
                pltpu.VMEM((nseq * nsub, blk, hk), F32), pltpu.VMEM((nseq * nsub, blk, hv), F32),
                pltpu.VMEM((nseq * nsub, blk, hv), F32)],
        ),
        out_shape=[jax.ShapeDtypeStruct((nb, t, d), F32),
                   jax.ShapeDtypeStruct((nb, N_HEADS, dk, dv), F32),
                   jax.ShapeDtypeStruct((nb, N_HEADS, dk, dv), F32)],
        compiler_params=pltpu.CompilerParams(
            dimension_semantics=("parallel", "arbitrary"), vmem_limit_bytes=VMEM_LIMIT),
        name="attn",
    )(jnp.asarray([layer], jnp.int32), *args)


def _mix_tables(pos, blk, n_valid, dk, dv):
    half = dk // 2
    inv = ROPE_BASE ** (-jnp.arange(half, dtype=F32) / half)
    ang = pos.astype(F32)[..., None] * inv
    cos = jnp.cos(ang)
    sin = jnp.sin(ang)
    log_gamma = jnp.log1p(-(2.0 ** (-5.0 - jnp.arange(N_HEADS, dtype=F32))))
    step = jnp.arange(blk, dtype=F32)[:, None, None] + 1.0
    dec = jnp.broadcast_to(step * log_gamma[None, :, None], (blk, N_HEADS, dk)).reshape(blk, N_HEADS * dk)
    gpow = jnp.exp(n_valid.astype(F32)[:, None, None] * log_gamma[None, :, None])
    gpow = jnp.broadcast_to(gpow, (pos.shape[0], N_HEADS, dv)).reshape(pos.shape[0], 1, N_HEADS * dv)
    return {
        "cos": jnp.concatenate([cos, cos], axis=-1),
        "sin": jnp.concatenate([-sin, sin], axis=-1),
        "decq": jnp.exp(dec),
        "deck": jnp.exp(-dec) * dk ** -0.5,
        "gpow": gpow,
    }


def kernel(x_prompt, x_sample, state_gla, state_ret, meta, ln_g, ln_b, w_ffn1_up, w_ffn1_down, w_in,
           w_alpha2, b_alpha, b_merge, gn_gla, gn_ret, w_o_gla, w_o_ret, w_out, w_ffn2_up, w_ffn2_down):
    nb, seq, d = x_prompt.shape
    nbs, seq_s, _ = x_sample.shape
    depth = ln_g.shape[0]
    n_meta = meta.shape[0]
    _, _, _, dk, dv = state_gla.shape
    hk, hv = N_HEADS * dk, N_HEADS * dv
    rank = w_alpha2.shape[1]
    alpha = (2.0 * depth) ** 0.25
    blk = min(MAIN_BLOCK, seq)
    sblk = SMALL_BLOCK
    nseq = MAIN_SEQS if nb % MAIN_SEQS == 0 else 1
    assert n_meta <= sblk and seq_s <= sblk

    hs = jnp.zeros((1 + nbs, sblk, d), F32)
    hs = hs.at[0, :n_meta].set(meta.astype(F32)).at[1:, :seq_s].set(x_sample)
    n_valid_s = jnp.asarray([n_meta] + [seq_s] * nbs, jnp.int32)
    valid_s = (jnp.arange(sblk)[None, :, None] < n_valid_s[:, None, None]).astype(F32)
    valid_s = jnp.broadcast_to(valid_s, (1 + nbs, sblk, dk))
    off = jnp.asarray([0] + [n_meta + PAST_LEN] * nbs, jnp.int32)
    tabs_s = _mix_tables(off[:, None] + jnp.arange(sblk)[None, :], sblk, n_valid_s, dk, dv)
    tabs_p = _mix_tables(n_meta + jnp.arange(seq)[None, :], blk, jnp.asarray([blk], jnp.int32), dk, dv)
    hp = x_prompt

    c_rg = 2 * hk + hv
    c_ag = c_rg + hv
    c_qr = c_ag + rank
    c_gr = c_qr + 2 * hk + hv
    c_mg = c_gr + hv
    w = {
        "mix": jnp.concatenate([w_in[:, :, :c_rg], w_in[:, :, c_qr:c_gr]], axis=2).astype(BF16),
        "ag": jnp.pad(w_in[:, :, c_ag:c_qr], ((0, 0), (0, 0), (0, GATE_RANK_PAD - rank))).astype(BF16),
        "a2": jnp.pad(w_alpha2, ((0, 0), (0, GATE_RANK_PAD - rank), (0, 0))).astype(BF16),
        "g": jnp.concatenate([w_in[:, :, c_rg:c_ag], w_in[:, :, c_gr:c_mg], w_in[:, :, c_mg:]],
                             axis=2).astype(BF16),
        "oa": w_o_gla.astype(BF16),
        "ob": w_o_ret.astype(BF16),
        "o": w_out.astype(BF16),
    }
    w1u, w1d = w_ffn1_up.astype(BF16), w_ffn1_down.astype(BF16)
    w2u, w2d = w_ffn2_up.astype(BF16), w_ffn2_down.astype(BF16)

    gla_p, ret_p, gla_s, ret_s = [], [], [], []
    for l in range(depth):
        vec = {
            "ba": b_alpha[l][None, :],
            "gna": gn_gla[l][None, :],
            "gnr": gn_ret[l][None, :],
            "bm": b_merge[l][None, :],
            "lng": ln_g[l, 1][None, :],
            "lnb": ln_b[l, 1][None, :],
        }

        def layer(h3, gla0, ret0, tabs, bsz, ns, valid):
            shape = h3.shape
            h = _ffn_ln(h3.reshape(-1, d), w1u, w1d, l, ln_g[l, 0][None, :], ln_b[l, 0][None, :], alpha)
            h, sg, sr = _attn(h.reshape(shape), gla0, ret0, w, l, vec, tabs, bsz, ns, alpha, valid)
            h = _ffn_ln(h.reshape(-1, d), w2u, w2d, l, ln_g[l, 2][None, :], ln_b[l, 2][None, :], alpha)
            return h.reshape(shape), sg, sr

        zero = jnp.zeros((1, N_HEADS, dk, dv), F32)
        hs, sg_s, sr_s = layer(hs, jnp.concatenate([zero, state_gla[l].astype(F32)], axis=0),
                               jnp.concatenate([zero, state_ret[l].astype(F32)], axis=0),
                               tabs_s, sblk, 1, valid_s)
        hp, sg_p, sr_p = layer(hp, sg_s[:1], sr_s[:1], tabs_p, blk, nseq, None)
        gla_p.append(sg_p.astype(x_prompt.dtype))
        ret_p.append(sr_p.astype(x_prompt.dtype))
        gla_s.append(sg_s[1:].astype(state_gla.dtype))
        ret_s.append(sr_s[1:].astype(state_ret.dtype))

    return (hp, hs[1:, :seq_s], jnp.stack(gla_p, axis=0), jnp.stack(ret_p, axis=0),
            jnp.stack(gla_s, axis=0), jnp.stack(ret_s, axis=0))
```

```python
import functools

import numpy as np
import jax
import jax.numpy as jnp
from jax import lax
from jax.experimental import pallas as pl
from jax.experimental.pallas import tpu as pltpu

F32 = jnp.float32
BF16 = jnp.bfloat16

N_HEADS = 4
GATE_RANK_PAD = 128
GATE_TAU = 16.0
PAST_LEN = 1024
ROPE_BASE = 10000.0
LN_EPS = 1e-5
GN_EPS = 1e-5
LOG2E = 1.4426950408889634

SUBLANES = 8
MXU_WIDTH = 256
MAIN_BLOCK = 256
MAIN_SEQS = 1
MAIN_SUBBLOCKS = 4
STAGGER = 26
DIAG_BLOCK = 128
GATE_SLOTS = (1, 1, (1, 0, 1, 0, 1, 0, 1, 0), 0, (1, 1, 0, 0), 0)
SMALL_BLOCK = 32
ROW_TILE = 1024
ROW_SUB = 256
FFN_CHUNK = 6 * MXU_WIDTH
VMEM_LIMIT = 60 * 1024 * 1024

_NT = (((1,), (1,)), ((), ()))
_TN = (((0,), (0,)), ((), ()))


def _dot(a, b):
    return jnp.dot(a, b, preferred_element_type=F32)


def _dot_nt(a, b):
    return lax.dot_general(a, b, _NT, preferred_element_type=F32)


def _dot_tn(a, b):
    return lax.dot_general(a, b, _TN, preferred_element_type=F32)


def _layer_norm(x, g, b):
    mu = jnp.mean(x, axis=-1, keepdims=True)
    xc = x - mu
    var = jnp.mean(xc * xc, axis=-1, keepdims=True)
    return xc * lax.rsqrt(var + LN_EPS) * g + b


def _sigmoid(x):
    return 0.5 + 0.5 * jnp.tanh(0.5 * x)


def _silu(x):
    hx = 0.5 * x
    return hx + hx * jnp.tanh(hx)


def _log_sigmoid(x):
    return jnp.minimum(x, 0.0) - jnp.log1p(jnp.exp(-jnp.abs(x)))


def _resident(shape):
    return pl.BlockSpec(shape, lambda *_: (0,) * len(shape), pipeline_mode=pl.Buffered(1))


def _layer_of(stacked):
    rest = stacked.shape[1:]
    return pl.BlockSpec((None,) + rest, lambda *a: (a[-1][0],) + (0,) * len(rest),
                        pipeline_mode=pl.Buffered(1))


def _row_tile(n_rows, limit):
    t = min(limit, n_rows)
    while n_rows % t or t % SUBLANES:
        t -= SUBLANES
    return t


def _ffn_ln_kernel(_layer_ref, h_ref, wup_ref, wdn_ref, g_ref, b_ref, o_ref, *, d_ff, cuts, sub, alpha):
    n_sub = h_ref.shape[0] // sub
    n_ch = len(cuts) - 1
    items = [(s, j) for s in range(n_sub) for j in range(n_ch)]
    hb = [None] * n_sub
    acc = [None] * n_sub

    def up(s, j):
        if hb[s] is None:
            hb[s] = h_ref[s * sub:(s + 1) * sub, :].astype(BF16)
        return (_dot(hb[s], wup_ref[:, cuts[j]:cuts[j + 1]]),
                _dot(hb[s], wup_ref[:, d_ff + cuts[j]:d_ff + cuts[j + 1]]))

    def finish(s):
        rows = slice(s * sub, (s + 1) * sub)
        o_ref[rows, :] = _layer_norm(alpha * h_ref[rows, :] + 0.5 * acc[s], g_ref[...], b_ref[...])

    nxt = up(*items[0])
    ready = None
    for idx, (s, j) in enumerate(items):
        a, b = nxt
        if idx + 1 < len(items):
            nxt = up(*items[idx + 1])
        if ready is not None:
            finish(ready)
            ready = None
        y = _dot((_silu(a) * b).astype(BF16), wdn_ref[cuts[j]:cuts[j + 1], :])
        acc[s] = y if acc[s] is None else acc[s] + y
        if j == n_ch - 1:
            ready = s
    finish(ready)


def _ffn_ln(h, w_up, w_down, layer, g, b, alpha):
    n, d = h.shape
    d_ff = w_down.shape[1]
    cuts = tuple(range(0, d_ff, FFN_CHUNK)) + (d_ff,)
    tm = _row_tile(n, ROW_TILE)
    sub = ROW_SUB if tm % ROW_SUB == 0 else tm
    return pl.pallas_call(
        functools.partial(_ffn_ln_kernel, d_ff=d_ff, cuts=cuts, sub=sub, alpha=alpha),
        grid_spec=pltpu.PrefetchScalarGridSpec(
            num_scalar_prefetch=1,
            grid=(n // tm,),
            in_specs=[
                pl.BlockSpec((tm, d), lambda i, l: (i, 0)),
                _layer_of(w_up),
                _layer_of(w_down),
                _resident(g.shape),
                _resident(b.shape),
            ],
            out_specs=pl.BlockSpec((tm, d), lambda i, l: (i, 0)),
        ),
        out_shape=jax.ShapeDtypeStruct((n, d), F32),
        compiler_params=pltpu.CompilerParams(
            dimension_semantics=("parallel",), vmem_limit_bytes=VMEM_LIMIT),
        name="ffn_ln",
    )(jnp.asarray([layer], jnp.int32), h, w_up, w_down, g, b)


def _head_norm(o, gain):
    mu = jnp.mean(o, axis=-1, keepdims=True)
    oc = o - mu
    var = jnp.mean(oc * oc, axis=-1, keepdims=True)
    return oc * lax.rsqrt(var + GN_EPS) * gain


def _attn_kernel(_layer_ref, *refs, nseq, nsub, stagger, blk, dblk, dk, dv, masked, alpha):
    n_in = 24 if masked else 23
    (h_ref, wmix_ref, wag_ref, wa2_ref, ba_ref, cos_ref, sin_ref, decq_ref, deck_ref, tri_ref,
     lvl_ref, gla0_ref, ret0_ref, gna_ref, gnr_ref, gpow_ref, wg_ref, bm_ref, woa_ref, wob_ref,
     wo_ref, lng_ref, lnb_ref) = refs[:23]
    valid_ref = refs[23] if masked else None
    o_ref, sg_ref, sr_ref, sgt_scr, sr_scr, b_scr, oa_scr, ob_scr = refs[n_in:]
    hk = N_HEADS * dk
    hv = N_HEADS * dv
    d = h_ref.shape[-1]
    n_diag = blk // dblk
    lv_diag = dblk.bit_length() - 1
    n_lv = blk.bit_length() - 1
    c0 = 2 * hk + hv
    pw = d // 2
    n_gate = 4 * d // pw
    n_stages = 3 + n_lv + 2 + lv_diag + 2 * N_HEADS + 1
    assert nseq == 1 and 2 * stagger >= n_stages or nsub <= 2
    j = pl.program_id(1)

    def shared(ref, s):
        return min(s, ref.shape[0] - 1)

    @pl.when(j == 0)
    def _():
        for s in range(nseq):
            for hd in range(N_HEADS):
                sgt_scr[s, hd] = gla0_ref[shared(gla0_ref, s), hd].T
            sr_scr[s] = ret0_ref[shared(ret0_ref, s)]

    lvl = lvl_ref[...]
    r8 = lax.broadcasted_iota(jnp.int32, (blk, hk), 0) & (SUBLANES - 1)
    causal = (lax.broadcasted_iota(jnp.int32, (blk, blk), 0)
              >= lax.broadcasted_iota(jnp.int32, (blk, blk), 1))

    def tile(x, i, hd):
        return x[i * dblk:(i + 1) * dblk, hd * dk:(hd + 1) * dk]

    def block(s, u):
        rows = slice(u * blk, (u + 1) * blk)
        su = s * min(nsub, 2) + u % 2
        h = h_ref[s, rows, :]
        hb = h.astype(BF16)
        gates = []

        def gate_pieces(n):
            for _ in range(n):
                p = len(gates)
                if p == n_gate:
                    return
                x = _dot(hb, wg_ref[:, p * pw:(p + 1) * pw])
                if p * pw < 2 * d:
                    gates.append(_silu(x))
                else:
                    gates.append(_sigmoid(x + bm_ref[:, p * pw - 2 * d:(p + 1) * pw - 2 * d]))

        def slot(k, i):
            n = GATE_SLOTS[k]
            return n if isinstance(n, int) else (n[i] if i < len(n) else 0)

        ag = _dot(hb, wag_ref[...])
        qg = _dot(hb, wmix_ref[:, 0:hk]) * (dk ** -0.5)
        yield
        gate_in = _dot(ag.astype(BF16), wa2_ref[...])
        kg = _dot(hb, wmix_ref[:, hk:2 * hk])
        vg = _dot(hb, wmix_ref[:, 2 * hk:2 * hk + hv]).astype(BF16)
        gate_pieces(slot(0, 0))
        lg = _log_sigmoid(gate_in + ba_ref[...]) / GATE_TAU
        if masked:
            valid = valid_ref[s, rows, :]
            vmask = jnp.concatenate([valid] * N_HEADS, axis=1)
            lg = lg * vmask
            kg = kg * vmask
        yield
        qr = _dot(hb, wmix_ref[:, c0:c0 + hk])
        kr = _dot(hb, wmix_ref[:, c0 + hk:c0 + 2 * hk])
        lg_hi = lg.astype(BF16)
        rem = lg - lg_hi.astype(F32)
        lg_mid = rem.astype(BF16)
        lg_lo = (rem - lg_mid.astype(F32)).astype(BF16)
        tri = tri_ref[...]
        b2 = (_dot(tri, lg_hi) + _dot(tri, lg_mid) + _dot(tri, lg_lo)) * LOG2E
        b_scr[su] = b2
        vr = _dot(hb, wmix_ref[:, c0 + 2 * hk:c0 + 2 * hk + hv]).astype(BF16)
        gate_pieces(slot(1, 0))
        yield

        def in_group_rows(r):
            return jnp.concatenate(
                [jnp.broadcast_to(b_scr[su, pl.ds(SUBLANES * i + r, 1), :], (SUBLANES, hk))
                 for i in range(blk // SUBLANES)], axis=0)

        def level_factor(lv):
            if lv == 0:
                a = jnp.where(r8 < 2, in_group_rows(0),
                              jnp.where(r8 < 4, in_group_rows(2),
                                        jnp.where(r8 < 6, in_group_rows(4), in_group_rows(6))))
                diff = b2 - a
            elif lv == 1:
                diff = b2 - jnp.where(r8 < 4, in_group_rows(1), in_group_rows(5))
            elif lv == 2:
                diff = b2 - in_group_rows(3)
            else:
                grp = 2 << lv
                diff = jnp.concatenate(
                    [b2[g * grp:(g + 1) * grp] - b_scr[su, pl.ds(g * grp + grp // 2 - 1, 1), :]
                     for g in range(blk // grp)], axis=0)
            return jnp.exp2(-jnp.abs(diff))

        fac = []
        for lv in range(n_lv):
            gate_pieces(slot(2, lv))
            fac.append(level_factor(lv))
            yield
        last = b_scr[su, pl.ds(blk - 1, 1), :]
        q_in = (qg * jnp.exp2(b2)).astype(BF16)
        k_out = (kg * jnp.exp2(-jnp.abs(last - b2))).astype(BF16)
        s_dec = jnp.exp2(last)
        yield

        qgb = qg.astype(BF16)
        kgb = kg.astype(BF16)
        on_diag = lvl == lv_diag
        att = [[jnp.where(on_diag, _dot_nt(tile(qgb, i, hd), tile(kgb, i, hd)), 0.0)
                for hd in range(N_HEADS)] for i in range(n_diag)]
        yield
        for lv in range(lv_diag):
            qf = (qg * fac[lv]).astype(BF16)
            kf = (kg * fac[lv]).astype(BF16)
            at_level = lvl == lv
            for i in range(n_diag):
                for hd in range(N_HEADS):
                    att[i][hd] = jnp.where(at_level, _dot_nt(tile(qf, i, hd), tile(kf, i, hd)), att[i][hd])
            gate_pieces(slot(3, lv))
            yield
        if n_diag == 2:
            qf = (qg[dblk:] * fac[lv_diag][dblk:]).astype(BF16)
            kf = (kg[:dblk] * fac[lv_diag][:dblk]).astype(BF16)
        gna = gna_ref[...]
        for hd in range(N_HEADS):
            ks = slice(hd * dk, (hd + 1) * dk)
            vs = slice(hd * dv, (hd + 1) * dv)
            v = vg[:, vs]
            if n_diag == 2:
                below = _dot_nt(qf[:, ks], kf[:, ks])
                o = jnp.concatenate([
                    _dot(att[0][hd].astype(BF16), v[:dblk]),
                    _dot(jnp.concatenate([below, att[1][hd]], axis=1).astype(BF16), v)], axis=0)
            else:
                o = _dot(att[0][hd].astype(BF16), v)
            st = sgt_scr[s, hd]
            o = o + _dot_nt(q_in[:, ks], st.astype(BF16))
            sgt_scr[s, hd] = st * s_dec[:, ks] + _dot_tn(v, k_out[:, ks])
            oa_scr[su, :, vs] = _head_norm(o, gna[:, vs])
            gate_pieces(slot(4, hd))
            yield

        cos = cos_ref[shared(cos_ref, s), rows, :]
        sin = sin_ref[shared(sin_ref, s), rows, :]
        decq = decq_ref[...]
        deck = deck_ref[...]
        gpow = gpow_ref[shared(gpow_ref, s)]
        gnr = gnr_ref[...]
        for hd in range(N_HEADS):
            ks = slice(hd * dk, (hd + 1) * dk)
            vs = slice(hd * dv, (hd + 1) * dv)
            q = qr[:, ks]
            k = kr[:, ks]
            q = (q * cos + pltpu.roll(q, dk // 2, 1) * sin) * decq[:, ks]
            k = (k * cos + pltpu.roll(k, dk // 2, 1) * sin) * deck[:, ks]
            if masked:
                k = k * valid
            qb = q.astype(BF16)
            kb = k.astype(BF16)
            v = vr[:, vs]
            sc = jnp.where(causal, _dot_nt(qb, kb), 0.0)
            s0 = sr_scr[s, hd]
            o = _dot(sc.astype(BF16), v) + _dot(qb, s0.astype(BF16))
            sr_scr[s, hd] = (s0 + _dot_tn(kb, v)) * gpow[:, vs]
            ob_scr[su, :, vs] = _head_norm(o, gnr[:, vs])
            gate_pieces(slot(5, hd))
            yield

        gate_pieces(n_gate)
        rg, gr, ga, gb = [jnp.concatenate(gates[i * (d // pw):(i + 1) * (d // pw)], axis=1)
                          for i in range(4)]
        ya = _dot((rg * oa_scr[su]).astype(BF16), woa_ref[...])
        yb = _dot((gr * ob_scr[su]).astype(BF16), wob_ref[...])
        y = _dot((ga * ya + gb * yb).astype(BF16), wo_ref[...])
        o_ref[s, rows, :] = _layer_norm(alpha * h + y, lng_ref[...], lnb_ref[...])
        yield

    waiting = [block(s, u) for u in range(nsub) for s in range(nseq)]
    running = []
    tick = 0
    while waiting or running:
        if waiting and tick % stagger == 0:
            running.append(waiting.pop(0))
        for g in list(running):
            try:
                next(g)
            except StopIteration:
                running.remove(g)
        tick += 1

    @pl.when(j == pl.num_programs(1) - 1)
    def _():
        for s in range(nseq):
            for hd in range(N_HEADS):
                sg_ref[s, hd] = sgt_scr[s, hd].T
            sr_ref[s] = sr_scr[s]


def _level_matrix(n):
    r = np.arange(n)[:, None]
    c = np.arange(n)[None, :]
    x = np.maximum(r ^ c, 1)
    lvl = np.where(r > c, np.floor(np.log2(x)), np.where(r == c, n.bit_length() - 1, -1))
    return lvl.astype(np.int32)


def _attn(h, gla0, ret0, w, layer, vec, tabs, blk, nseq, alpha, valid=None):
    nb, t, d = h.shape
    _, _, dk, dv = gla0.shape
    hk, hv = N_HEADS * dk, N_HEADS * dv
    dblk = min(blk, DIAG_BLOCK)
    nsub = MAIN_SUBBLOCKS if (valid is None and t % (MAIN_SUBBLOCKS * blk) == 0) else 1
    tstep = nsub * blk
    assert t % tstep == 0 and blk & (blk - 1) == 0 and SUBLANES <= blk <= 256 and blk // dblk <= 2
    assert nb % nseq == 0
    lvl = jnp.asarray(_level_matrix(dblk))
    tri = jnp.asarray(np.tril(np.ones((blk, blk), np.float32)), BF16)
    masked = valid is not None

    def per_seq(a, width):
        if a.shape[0] > 1:
            return pl.BlockSpec((nseq, tstep, width), lambda b, j, l: (b, j, 0))
        return pl.BlockSpec((1, tstep, width), lambda b, j, l: (0, j, 0))

    def state_in(a):
        if a.shape[0] > 1:
            return pl.BlockSpec((nseq, N_HEADS, dk, dv), lambda b, j, l: (b, 0, 0, 0))
        return pl.BlockSpec((1, N_HEADS, dk, dv), lambda b, j, l: (0, 0, 0, 0))

    seq = pl.BlockSpec((nseq, tstep, d), lambda b, j, l: (b, j, 0))
    args = [h, w["mix"], w["ag"], w["a2"], vec["ba"], tabs["cos"], tabs["sin"], tabs["decq"], tabs["deck"],
            tri, lvl, gla0, ret0, vec["gna"], vec["gnr"], tabs["gpow"], w["g"], vec["bm"], w["oa"], w["ob"],
            w["o"], vec["lng"], vec["lnb"]]
    in_specs = [
        seq,
        _layer_of(w["mix"]), _layer_of(w["ag"]), _layer_of(w["a2"]),
        _resident(vec["ba"].shape),
        per_seq(tabs["cos"], dk), per_seq(tabs["sin"], dk),
        _resident(tabs["decq"].shape), _resident(tabs["deck"].shape),
        _resident(tri.shape), _resident(lvl.shape),
        state_in(gla0), state_in(ret0),
        _resident(vec["gna"].shape), _resident(vec["gnr"].shape),
        (pl.BlockSpec((nseq, 1, hv), lambda b, j, l: (b, 0, 0)) if tabs["gpow"].shape[0] > 1
         else pl.BlockSpec((1, 1, hv), lambda b, j, l: (0, 0, 0))),
        _layer_of(w["g"]), _resident(vec["bm"].shape), _layer_of(w["oa"]),
        _layer_of(w["ob"]), _layer_of(w["o"]), _resident(vec["lng"].shape),
        _resident(vec["lnb"].shape),
    ]
    if masked:
        in_specs.append(pl.BlockSpec((nseq, tstep, dk), lambda b, j, l: (b, j, 0)))
        args.append(valid)
    state_out = pl.BlockSpec((nseq, N_HEADS, dk, dv), lambda b, j, l: (b, 0, 0, 0))
    return pl.pallas_call(
        functools.partial(_attn_kernel, nseq=nseq, nsub=nsub, stagger=STAGGER, blk=blk, dblk=dblk, dk=dk, dv=dv,
                          masked=masked, alpha=alpha),
        grid_spec=pltpu.PrefetchScalarGridSpec(
            num_scalar_prefetch=1,
            grid=(nb // nseq, t // tstep),
            in_specs=in_specs,
            out_specs=[seq, state_out, state_out],
            scratch_shapes=[
                pltpu.VMEM((nseq, N_HEADS, dv, dk), F32), pltpu.VMEM((nseq, N_HEADS, dk, dv), F32),
                pltpu.VMEM((nseq * min(nsub, 2), blk, hk), F32),
                pltpu.VMEM((nseq * min(nsub, 2), blk, hv), F32),
                pltpu.VMEM((nseq * min(nsub, 2), blk, hv), F32)],
        ),
        out_shape=[jax.ShapeDtypeStruct((nb, t, d), F32),
                   jax.ShapeDtypeStruct((nb, N_HEADS, dk, dv), F32),
                   jax.ShapeDtypeStruct((nb, N_HEADS, dk, dv), F32)],
        compiler_params=pltpu.CompilerParams(
            dimension_semantics=("parallel", "arbitrary"), vmem_limit_bytes=VMEM_LIMIT),
        name="attn",
    )(jnp.asarray([layer], jnp.int32), *args)


def _mix_tables(pos, blk, n_valid, dk, dv):
    half = dk // 2
    inv = ROPE_BASE ** (-jnp.arange(half, dtype=F32) / half)
    ang = pos.astype(F32)[..., None] * inv
    cos = jnp.cos(ang)
    sin = jnp.sin(ang)
    log_gamma = jnp.log1p(-(2.0 ** (-5.0 - jnp.arange(N_HEADS, dtype=F32))))
    step = jnp.arange(blk, dtype=F32)[:, None, None] + 1.0
    dec = jnp.broadcast_to(step * log_gamma[None, :, None], (blk, N_HEADS, dk)).reshape(blk, N_HEADS * dk)
    gpow = jnp.exp(n_valid.astype(F32)[:, None, None] * log_gamma[None, :, None])
    gpow = jnp.broadcast_to(gpow, (pos.shape[0], N_HEADS, dv)).reshape(pos.shape[0], 1, N_HEADS * dv)
    return {
        "cos": jnp.concatenate([cos, cos], axis=-1),
        "sin": jnp.concatenate([-sin, sin], axis=-1),
        "decq": jnp.exp(dec),
        "deck": jnp.exp(-dec) * dk ** -0.5,
        "gpow": gpow,
    }


def kernel(x_prompt, x_sample, state_gla, state_ret, meta, ln_g, ln_b, w_ffn1_up, w_ffn1_down, w_in,
           w_alpha2, b_alpha, b_merge, gn_gla, gn_ret, w_o_gla, w_o_ret, w_out, w_ffn2_up, w_ffn2_down):
    nb, seq, d = x_prompt.shape
    nbs, seq_s, _ = x_sample.shape
    depth = ln_g.shape[0]
    n_meta = meta.shape[0]
    _, _, _, dk, dv = state_gla.shape
    hk, hv = N_HEADS * dk, N_HEADS * dv
    rank = w_alpha2.shape[1]
    alpha = (2.0 * depth) ** 0.25
    blk = min(MAIN_BLOCK, seq)
    sblk = SMALL_BLOCK
    nseq = MAIN_SEQS if nb % MAIN_SEQS == 0 else 1
    assert n_meta <= sblk and seq_s <= sblk

    hs = jnp.zeros((1 + nbs, sblk, d), F32)
    hs = hs.at[0, :n_meta].set(meta.astype(F32)).at[1:, :seq_s].set(x_sample)
    n_valid_s = jnp.asarray([n_meta] + [seq_s] * nbs, jnp.int32)
    valid_s = (jnp.arange(sblk)[None, :, None] < n_valid_s[:, None, None]).astype(F32)
    valid_s = jnp.broadcast_to(valid_s, (1 + nbs, sblk, dk))
    off = jnp.asarray([0] + [n_meta + PAST_LEN] * nbs, jnp.int32)
    tabs_s = _mix_tables(off[:, None] + jnp.arange(sblk)[None, :], sblk, n_valid_s, dk, dv)
    tabs_p = _mix_tables(n_meta + jnp.arange(seq)[None, :], blk, jnp.asarray([blk], jnp.int32), dk, dv)
    hp = x_prompt

    c_rg = 2 * hk + hv
    c_ag = c_rg + hv
    c_qr = c_ag + rank
    c_gr = c_qr + 2 * hk + hv
    c_mg = c_gr + hv
    w = {
        "mix": jnp.concatenate([w_in[:, :, :c_rg], w_in[:, :, c_qr:c_gr]], axis=2).astype(BF16),
        "ag": jnp.pad(w_in[:, :, c_ag:c_qr], ((0, 0), (0, 0), (0, GATE_RANK_PAD - rank))).astype(BF16),
        "a2": jnp.pad(w_alpha2, ((0, 0), (0, GATE_RANK_PAD - rank), (0, 0))).astype(BF16),
        "g": jnp.concatenate([w_in[:, :, c_rg:c_ag], w_in[:, :, c_gr:c_mg], w_in[:, :, c_mg:]],
                             axis=2).astype(BF16),
        "oa": w_o_gla.astype(BF16),
        "ob": w_o_ret.astype(BF16),
        "o": w_out.astype(BF16),
    }
    w1u, w1d = w_ffn1_up.astype(BF16), w_ffn1_down.astype(BF16)
    w2u, w2d = w_ffn2_up.astype(BF16), w_ffn2_down.astype(BF16)

    gla_p, ret_p, gla_s, ret_s = [], [], [], []
    for l in range(depth):
        vec = {
            "ba": b_alpha[l][None, :],
            "gna": gn_gla[l][None, :],
            "gnr": gn_ret[l][None, :],
            "bm": b_merge[l][None, :],
            "lng": ln_g[l, 1][None, :],
            "lnb": ln_b[l, 1][None, :],
        }

        def layer(h3, gla0, ret0, tabs, bsz, ns, valid):
            shape = h3.shape
            h = _ffn_ln(h3.reshape(-1, d), w1u, w1d, l, ln_g[l, 0][None, :], ln_b[l, 0][None, :], alpha)
            h, sg, sr = _attn(h.reshape(shape), gla0, ret0, w, l, vec, tabs, bsz, ns, alpha, valid)
            h = _ffn_ln(h.reshape(-1, d), w2u, w2d, l, ln_g[l, 2][None, :], ln_b[l, 2][None, :], alpha)
            return h.reshape(shape), sg, sr

        zero = jnp.zeros((1, N_HEADS, dk, dv), F32)
        hs, sg_s, sr_s = layer(hs, jnp.concatenate([zero, state_gla[l].astype(F32)], axis=0),
                               jnp.concatenate([zero, state_ret[l].astype(F32)], axis=0),
                               tabs_s, sblk, 1, valid_s)
        hp, sg_p, sr_p = layer(hp, sg_s[:1], sr_s[:1], tabs_p, blk, nseq, None)
        gla_p.append(sg_p.astype(x_prompt.dtype))
        ret_p.append(sr_p.astype(x_prompt.dtype))
        gla_s.append(sg_s[1:].astype(state_gla.dtype))
        ret_s.append(sr_s[1:].astype(state_ret.dtype))

    return (hp, hs[1:, :seq_s], jnp.stack(gla_p, axis=0), jnp.stack(ret_p, axis=0),
            jnp.stack(gla_s, axis=0), jnp.stack(ret_s, axis=0))
```

```python
import functools

import numpy as np
import jax
import jax.numpy as jnp
from jax import lax
from jax.experimental import pallas as pl
from jax.experimental.pallas import tpu as pltpu

F32 = jnp.float32
BF16 = jnp.bfloat16

N_HEADS = 4
GATE_RANK_PAD = 128
GATE_TAU = 16.0
PAST_LEN = 1024
ROPE_BASE = 10000.0
LN_EPS = 1e-5
GN_EPS = 1e-5
LOG2E = 1.4426950408889634

SUBLANES = 8
MXU_WIDTH = 256
MAIN_BLOCK = 256
MAIN_SEQS = 1
MAIN_SUBBLOCKS = 2
STAGGER = 26
DIAG_BLOCK = 128
GATE_SLOTS = (2, 2, (1, 1, 1, 1), 0, 0, 0)
SMALL_BLOCK = 32
ROW_TILE = 1024
ROW_SUB = 512
FFN_CHUNK = 6 * MXU_WIDTH
VMEM_LIMIT = 56 * 1024 * 1024

_NT = (((1,), (1,)), ((), ()))
_TN = (((0,), (0,)), ((), ()))


def _dot(a, b):
    return jnp.dot(a, b, preferred_element_type=F32)


def _dot_nt(a, b):
    return lax.dot_general(a, b, _NT, preferred_element_type=F32)


def _dot_tn(a, b):
    return lax.dot_general(a, b, _TN, preferred_element_type=F32)


def _layer_norm(x, g, b):
    mu = jnp.mean(x, axis=-1, keepdims=True)
    xc = x - mu
    var = jnp.mean(xc * xc, axis=-1, keepdims=True)
    return xc * lax.rsqrt(var + LN_EPS) * g + b


def _sigmoid(x):
    return 0.5 + 0.5 * jnp.tanh(0.5 * x)


def _silu(x):
    hx = 0.5 * x
    return hx + hx * jnp.tanh(hx)


def _log_sigmoid(x):
    return jnp.minimum(x, 0.0) - jnp.log1p(jnp.exp(-jnp.abs(x)))


def _resident(shape):
    return pl.BlockSpec(shape, lambda *_: (0,) * len(shape), pipeline_mode=pl.Buffered(1))


def _layer_of(stacked):
    rest = stacked.shape[1:]
    return pl.BlockSpec((None,) + rest, lambda *a: (a[-1][0],) + (0,) * len(rest),
                        pipeline_mode=pl.Buffered(1))


def _row_tile(n_rows, limit):
    t = min(limit, n_rows)
    while n_rows % t or t % SUBLANES:
        t -= SUBLANES
    return t


def _ffn_ln_kernel(_layer_ref, h_ref, wup_ref, wdn_ref, g_ref, b_ref, o_ref, *, d_ff, cuts, sub, alpha):
    n_sub = h_ref.shape[0] // sub
    n_ch = len(cuts) - 1
    items = [(s, j) for s in range(n_sub) for j in range(n_ch)]
    hb = [None] * n_sub
    acc = [None] * n_sub

    def up(s, j):
        if hb[s] is None:
            hb[s] = h_ref[s * sub:(s + 1) * sub, :].astype(BF16)
        return (_dot(hb[s], wup_ref[:, cuts[j]:cuts[j + 1]]),
                _dot(hb[s], wup_ref[:, d_ff + cuts[j]:d_ff + cuts[j + 1]]))

    def finish(s):
        rows = slice(s * sub, (s + 1) * sub)
        o_ref[rows, :] = _layer_norm(alpha * h_ref[rows, :] + 0.5 * acc[s], g_ref[...], b_ref[...])

    nxt = up(*items[0])
    ready = None
    for idx, (s, j) in enumerate(items):
        a, b = nxt
        if idx + 1 < len(items):
            nxt = up(*items[idx + 1])
        if ready is not None:
            finish(ready)
            ready = None
        y = _dot((_silu(a) * b).astype(BF16), wdn_ref[cuts[j]:cuts[j + 1], :])
        acc[s] = y if acc[s] is None else acc[s] + y
        if j == n_ch - 1:
            ready = s
    finish(ready)


def _ffn_ln(h, w_up, w_down, layer, g, b, alpha):
    n, d = h.shape
    d_ff = w_down.shape[1]
    cuts = tuple(range(0, d_ff, FFN_CHUNK)) + (d_ff,)
    tm = _row_tile(n, ROW_TILE)
    sub = ROW_SUB if tm % ROW_SUB == 0 else tm
    return pl.pallas_call(
        functools.partial(_ffn_ln_kernel, d_ff=d_ff, cuts=cuts, sub=sub, alpha=alpha),
        grid_spec=pltpu.PrefetchScalarGridSpec(
            num_scalar_prefetch=1,
            grid=(n // tm,),
            in_specs=[
                pl.BlockSpec((tm, d), lambda i, l: (i, 0)),
                _layer_of(w_up),
                _layer_of(w_down),
                _resident(g.shape),
                _resident(b.shape),
            ],
            out_specs=pl.BlockSpec((tm, d), lambda i, l: (i, 0)),
        ),
        out_shape=jax.ShapeDtypeStruct((n, d), F32),
        compiler_params=pltpu.CompilerParams(
            dimension_semantics=("parallel",), vmem_limit_bytes=VMEM_LIMIT),
        name="ffn_ln",
    )(jnp.asarray([layer], jnp.int32), h, w_up, w_down, g, b)


def _head_norm(o, gain):
    mu = jnp.mean(o, axis=-1, keepdims=True)
    oc = o - mu
    var = jnp.mean(oc * oc, axis=-1, keepdims=True)
    return oc * lax.rsqrt(var + GN_EPS) * gain


def _attn_kernel(_layer_ref, *refs, nseq, nsub, stagger, blk, dblk, dk, dv, masked, alpha):
    n_in = 24 if masked else 23
    (h_ref, wmix_ref, wag_ref, wa2_ref, ba_ref, cos_ref, sin_ref, decq_ref, deck_ref, tri_ref,
     lvl_ref, gla0_ref, ret0_ref, gna_ref, gnr_ref, gpow_ref, wg_ref, bm_ref, woa_ref, wob_ref,
     wo_ref, lng_ref, lnb_ref) = refs[:23]
    valid_ref = refs[23] if masked else None
    o_ref, sg_ref, sr_ref, sgt_scr, sr_scr, b_scr, oa_scr, ob_scr = refs[n_in:]
    hk = N_HEADS * dk
    hv = N_HEADS * dv
    d = h_ref.shape[-1]
    n_diag = blk // dblk
    lv_diag = dblk.bit_length() - 1
    n_lv = blk.bit_length() - 1
    c0 = 2 * hk + hv
    pw = d // 2
    n_gate = 4 * d // pw
    j = pl.program_id(1)

    def shared(ref, s):
        return min(s, ref.shape[0] - 1)

    @pl.when(j == 0)
    def _():
        for s in range(nseq):
            for hd in range(N_HEADS):
                sgt_scr[s, hd] = gla0_ref[shared(gla0_ref, s), hd].T
            sr_scr[s] = ret0_ref[shared(ret0_ref, s)]

    lvl = lvl_ref[...]
    r8 = lax.broadcasted_iota(jnp.int32, (blk, hk), 0) & (SUBLANES - 1)
    causal = (lax.broadcasted_iota(jnp.int32, (blk, blk), 0)
              >= lax.broadcasted_iota(jnp.int32, (blk, blk), 1))

    def tile(x, i, hd):
        return x[i * dblk:(i + 1) * dblk, hd * dk:(hd + 1) * dk]

    def block(s, u):
        rows = slice(u * blk, (u + 1) * blk)
        su = s * nsub + u
        h = h_ref[s, rows, :]
        hb = h.astype(BF16)
        gates = []

        def gate_pieces(n):
            for _ in range(n):
                p = len(gates)
                if p == n_gate:
                    return
                x = _dot(hb, wg_ref[:, p * pw:(p + 1) * pw])
                if p * pw < 2 * d:
                    gates.append(_silu(x))
                else:
                    gates.append(_sigmoid(x + bm_ref[:, p * pw - 2 * d:(p + 1) * pw - 2 * d]))

        def slot(k, i):
            n = GATE_SLOTS[k]
            return n if isinstance(n, int) else (n[i] if i < len(n) else 0)

        ag = _dot(hb, wag_ref[...])
        qg = _dot(hb, wmix_ref[:, 0:hk]) * (dk ** -0.5)
        yield
        gate_in = _dot(ag.astype(BF16), wa2_ref[...])
        kg = _dot(hb, wmix_ref[:, hk:2 * hk])
        vg = _dot(hb, wmix_ref[:, 2 * hk:2 * hk + hv]).astype(BF16)
        gate_pieces(slot(0, 0))
        lg = _log_sigmoid(gate_in + ba_ref[...]) / GATE_TAU
        if masked:
            valid = valid_ref[s, rows, :]
            vmask = jnp.concatenate([valid] * N_HEADS, axis=1)
            lg = lg * vmask
            kg = kg * vmask
        yield
        qr = _dot(hb, wmix_ref[:, c0:c0 + hk])
        kr = _dot(hb, wmix_ref[:, c0 + hk:c0 + 2 * hk])
        lg_hi = lg.astype(BF16)
        rem = lg - lg_hi.astype(F32)
        lg_mid = rem.astype(BF16)
        lg_lo = (rem - lg_mid.astype(F32)).astype(BF16)
        tri = tri_ref[...]
        b2 = (_dot(tri, lg_hi) + _dot(tri, lg_mid) + _dot(tri, lg_lo)) * LOG2E
        b_scr[su] = b2
        vr = _dot(hb, wmix_ref[:, c0 + 2 * hk:c0 + 2 * hk + hv]).astype(BF16)
        gate_pieces(slot(1, 0))
        yield

        def in_group_rows(r):
            return jnp.concatenate(
                [jnp.broadcast_to(b_scr[su, pl.ds(SUBLANES * i + r, 1), :], (SUBLANES, hk))
                 for i in range(blk // SUBLANES)], axis=0)

        def level_factor(lv):
            if lv == 0:
                a = jnp.where(r8 < 2, in_group_rows(0),
                              jnp.where(r8 < 4, in_group_rows(2),
                                        jnp.where(r8 < 6, in_group_rows(4), in_group_rows(6))))
                diff = b2 - a
            elif lv == 1:
                diff = b2 - jnp.where(r8 < 4, in_group_rows(1), in_group_rows(5))
            elif lv == 2:
                diff = b2 - in_group_rows(3)
            else:
                grp = 2 << lv
                diff = jnp.concatenate(
                    [b2[g * grp:(g + 1) * grp] - b_scr[su, pl.ds(g * grp + grp // 2 - 1, 1), :]
                     for g in range(blk // grp)], axis=0)
            return jnp.exp2(-jnp.abs(diff))

        fac = []
        for lv in range(n_lv):
            gate_pieces(slot(2, lv))
            fac.append(level_factor(lv))
            yield
        last = b_scr[su, pl.ds(blk - 1, 1), :]
        q_in = (qg * jnp.exp2(b2)).astype(BF16)
        k_out = (kg * jnp.exp2(-jnp.abs(last - b2))).astype(BF16)
        s_dec = jnp.exp2(last)
        yield

        qgb = qg.astype(BF16)
        kgb = kg.astype(BF16)
        on_diag = lvl == lv_diag
        att = [[jnp.where(on_diag, _dot_nt(tile(qgb, i, hd), tile(kgb, i, hd)), 0.0)
                for hd in range(N_HEADS)] for i in range(n_diag)]
        yield
        for lv in range(lv_diag):
            qf = (qg * fac[lv]).astype(BF16)
            kf = (kg * fac[lv]).astype(BF16)
            at_level = lvl == lv
            for i in range(n_diag):
                for hd in range(N_HEADS):
                    att[i][hd] = jnp.where(at_level, _dot_nt(tile(qf, i, hd), tile(kf, i, hd)), att[i][hd])
            gate_pieces(slot(3, lv))
            yield
        if n_diag == 2:
            qf = (qg[dblk:] * fac[lv_diag][dblk:]).astype(BF16)
            kf = (kg[:dblk] * fac[lv_diag][:dblk]).astype(BF16)
        gna = gna_ref[...]
        for hd in range(N_HEADS):
            ks = slice(hd * dk, (hd + 1) * dk)
            vs = slice(hd * dv, (hd + 1) * dv)
            v = vg[:, vs]
            if n_diag == 2:
                below = _dot_nt(qf[:, ks], kf[:, ks])
                o = jnp.concatenate([
                    _dot(att[0][hd].astype(BF16), v[:dblk]),
                    _dot(jnp.concatenate([below, att[1][hd]], axis=1).astype(BF16), v)], axis=0)
            else:
                o = _dot(att[0][hd].astype(BF16), v)
            st = sgt_scr[s, hd]
            o = o + _dot_nt(q_in[:, ks], st.astype(BF16))
            sgt_scr[s, hd] = st * s_dec[:, ks] + _dot_tn(v, k_out[:, ks])
            oa_scr[su, :, vs] = _head_norm(o, gna[:, vs])
            gate_pieces(slot(4, hd))
            yield

        cos = cos_ref[shared(cos_ref, s), rows, :]
        sin = sin_ref[shared(sin_ref, s), rows, :]
        decq = decq_ref[...]
        deck = deck_ref[...]
        gpow = gpow_ref[shared(gpow_ref, s)]
        gnr = gnr_ref[...]
        for hd in range(N_HEADS):
            ks = slice(hd * dk, (hd + 1) * dk)
            vs = slice(hd * dv, (hd + 1) * dv)
            q = qr[:, ks]
            k = kr[:, ks]
            q = (q * cos + pltpu.roll(q, dk // 2, 1) * sin) * decq[:, ks]
            k = (k * cos + pltpu.roll(k, dk // 2, 1) * sin) * deck[:, ks]
            if masked:
                k = k * valid
            qb = q.astype(BF16)
            kb = k.astype(BF16)
            v = vr[:, vs]
            sc = jnp.where(causal, _dot_nt(qb, kb), 0.0)
            s0 = sr_scr[s, hd]
            o = _dot(sc.astype(BF16), v) + _dot(qb, s0.astype(BF16))
            sr_scr[s, hd] = (s0 + _dot_tn(kb, v)) * gpow[:, vs]
            ob_scr[su, :, vs] = _head_norm(o, gnr[:, vs])
            gate_pieces(slot(5, hd))
            yield

        gate_pieces(n_gate)
        rg, gr, ga, gb = [jnp.concatenate(gates[i * (d // pw):(i + 1) * (d // pw)], axis=1)
                          for i in range(4)]
        ya = _dot((rg * oa_scr[su]).astype(BF16), woa_ref[...])
        yb = _dot((gr * ob_scr[su]).astype(BF16), wob_ref[...])
        y = _dot((ga * ya + gb * yb).astype(BF16), wo_ref[...])
        o_ref[s, rows, :] = _layer_norm(alpha * h + y, lng_ref[...], lnb_ref[...])
        yield

    waiting = [block(s, u) for u in range(nsub) for s in range(nseq)]
    running = []
    tick = 0
    while waiting or running:
        if waiting and tick % stagger == 0:
            running.append(waiting.pop(0))
        for g in list(running):
            try:
                next(g)
            except StopIteration:
                running.remove(g)
        tick += 1

    @pl.when(j == pl.num_programs(1) - 1)
    def _():
        for s in range(nseq):
            for hd in range(N_HEADS):
                sg_ref[s, hd] = sgt_scr[s, hd].T
            sr_ref[s] = sr_scr[s]


def _level_matrix(n):
    r = np.arange(n)[:, None]
    c = np.arange(n)[None, :]
    x = np.maximum(r ^ c, 1)
    lvl = np.where(r > c, np.floor(np.log2(x)), np.where(r == c, n.bit_length() - 1, -1))
    return lvl.astype(np.int32)


def _attn(h, gla0, ret0, w, layer, vec, tabs, blk, nseq, alpha, valid=None):
    nb, t, d = h.shape
    _, _, dk, dv = gla0.shape
    hk, hv = N_HEADS * dk, N_HEADS * dv
    dblk = min(blk, DIAG_BLOCK)
    nsub = MAIN_SUBBLOCKS if (valid is None and t % (MAIN_SUBBLOCKS * blk) == 0) else 1
    tstep = nsub * blk
    assert t % tstep == 0 and blk & (blk - 1) == 0 and SUBLANES <= blk <= 256 and blk // dblk <= 2
    assert nb % nseq == 0
    lvl = jnp.asarray(_level_matrix(dblk))
    tri = jnp.asarray(np.tril(np.ones((blk, blk), np.float32)), BF16)
    masked = valid is not None

    def per_seq(a, width):
        if a.shape[0] > 1:
            return pl.BlockSpec((nseq, tstep, width), lambda b, j, l: (b, j, 0))
        return pl.BlockSpec((1, tstep, width), lambda b, j, l: (0, j, 0))

    def state_in(a):
        if a.shape[0] > 1:
            return pl.BlockSpec((nseq, N_HEADS, dk, dv), lambda b, j, l: (b, 0, 0, 0))
        return pl.BlockSpec((1, N_HEADS, dk, dv), lambda b, j, l: (0, 0, 0, 0))

    seq = pl.BlockSpec((nseq, tstep, d), lambda b, j, l: (b, j, 0))
    args = [h, w["mix"], w["ag"], w["a2"], vec["ba"], tabs["cos"], tabs["sin"], tabs["decq"], tabs["deck"],
            tri, lvl, gla0, ret0, vec["gna"], vec["gnr"], tabs["gpow"], w["g"], vec["bm"], w["oa"], w["ob"],
            w["o"], vec["lng"], vec["lnb"]]
    in_specs = [
        seq,
        _layer_of(w["mix"]), _layer_of(w["ag"]), _layer_of(w["a2"]),
        _resident(vec["ba"].shape),
        per_seq(tabs["cos"], dk), per_seq(tabs["sin"], dk),
        _resident(tabs["decq"].shape), _resident(tabs["deck"].shape),
        _resident(tri.shape), _resident(lvl.shape),
        state_in(gla0), state_in(ret0),
        _resident(vec["gna"].shape), _resident(vec["gnr"].shape),
        (pl.BlockSpec((nseq, 1, hv), lambda b, j, l: (b, 0, 0)) if tabs["gpow"].shape[0] > 1
         else pl.BlockSpec((1, 1, hv), lambda b, j, l: (0, 0, 0))),
        _layer_of(w["g"]), _resident(vec["bm"].shape), _layer_of(w["oa"]),
        _layer_of(w["ob"]), _layer_of(w["o"]), _resident(vec["lng"].shape),
        _resident(vec["lnb"].shape),
    ]
    if masked:
        in_specs.append(pl.BlockSpec((nseq, tstep, dk), lambda b, j, l: (b, j, 0)))
        args.append(valid)
    state_out = pl.BlockSpec((nseq, N_HEADS, dk, dv), lambda b, j, l: (b, 0, 0, 0))
    return pl.pallas_call(
        functools.partial(_attn_kernel, nseq=nseq, nsub=nsub, stagger=STAGGER, blk=blk, dblk=dblk, dk=dk, dv=dv,
                          masked=masked, alpha=alpha),
        grid_spec=pltpu.PrefetchScalarGridSpec(
            num_scalar_prefetch=1,
            grid=(nb // nseq, t // tstep),
            in_specs=in_specs,
            out_specs=[seq, state_out, state_out],
            scratch_shapes=[
                pltpu.VMEM((nseq, N_HEADS, dv, dk), F32), pltpu.VMEM((nseq, N_HEADS, dk, dv), F32),
                pltpu.VMEM((nseq * nsub, blk, hk), F32), pltpu.VMEM((nseq * nsub, blk, hv), F32),
                pltpu.VMEM((nseq * nsub, blk, hv), F32)],
        ),
        out_shape=[jax.ShapeDtypeStruct((nb, t, d), F32),
                   jax.ShapeDtypeStruct((nb, N_HEADS, dk, dv), F32),
                   jax.ShapeDtypeStruct((nb, N_HEADS, dk, dv), F32)],
        compiler_params=pltpu.CompilerParams(
            dimension_semantics=("parallel", "arbitrary"), vmem_limit_bytes=VMEM_LIMIT),
        name="attn",
    )(jnp.asarray([layer], jnp.int32), *args)


def _mix_tables(pos, blk, n_valid, dk, dv):
    half = dk // 2
    inv = ROPE_BASE ** (-jnp.arange(half, dtype=F32) / half)
    ang = pos.astype(F32)[..., None] * inv
    cos = jnp.cos(ang)
    sin = jnp.sin(ang)
    log_gamma = jnp.log1p(-(2.0 ** (-5.0 - jnp.arange(N_HEADS, dtype=F32))))
    step = jnp.arange(blk, dtype=F32)[:, None, None] + 1.0
    dec = jnp.broadcast_to(step * log_gamma[None, :, None], (blk, N_HEADS, dk)).reshape(blk, N_HEADS * dk)
    gpow = jnp.exp(n_valid.astype(F32)[:, None, None] * log_gamma[None, :, None])
    gpow = jnp.broadcast_to(gpow, (pos.shape[0], N_HEADS, dv)).reshape(pos.shape[0], 1, N_HEADS * dv)
    return {
        "cos": jnp.concatenate([cos, cos], axis=-1),
        "sin": jnp.concatenate([-sin, sin], axis=-1),
        "decq": jnp.exp(dec),
        "deck": jnp.exp(-dec) * dk ** -0.5,
        "gpow": gpow,
    }


def kernel(x_prompt, x_sample, state_gla, state_ret, meta, ln_g, ln_b, w_ffn1_up, w_ffn1_down, w_in,
           w_alpha2, b_alpha, b_merge, gn_gla, gn_ret, w_o_gla, w_o_ret, w_out, w_ffn2_up, w_ffn2_down):
    nb, seq, d = x_prompt.shape
    nbs, seq_s, _ = x_sample.shape
    depth = ln_g.shape[0]
    n_meta = meta.shape[0]
    _, _, _, dk, dv = state_gla.shape
    hk, hv = N_HEADS * dk, N_HEADS * dv
    rank = w_alpha2.shape[1]
    alpha = (2.0 * depth) ** 0.25
    blk = min(MAIN_BLOCK, seq)
    sblk = SMALL_BLOCK
    nseq = MAIN_SEQS if nb % MAIN_SEQS == 0 else 1
    assert n_meta <= sblk and seq_s <= sblk

    hs = jnp.zeros((1 + nbs, sblk, d), F32)
    hs = hs.at[0, :n_meta].set(meta.astype(F32)).at[1:, :seq_s].set(x_sample)
    n_valid_s = jnp.asarray([n_meta] + [seq_s] * nbs, jnp.int32)
    valid_s = (jnp.arange(sblk)[None, :, None] < n_valid_s[:, None, None]).astype(F32)
    valid_s = jnp.broadcast_to(valid_s, (1 + nbs, sblk, dk))
    off = jnp.asarray([0] + [n_meta + PAST_LEN] * nbs, jnp.int32)
    tabs_s = _mix_tables(off[:, None] + jnp.arange(sblk)[None, :], sblk, n_valid_s, dk, dv)
    tabs_p = _mix_tables(n_meta + jnp.arange(seq)[None, :], blk, jnp.asarray([blk], jnp.int32), dk, dv)
    hp = x_prompt

    c_rg = 2 * hk + hv
    c_ag = c_rg + hv
    c_qr = c_ag + rank
    c_gr = c_qr + 2 * hk + hv
    c_mg = c_gr + hv
    w = {
        "mix": jnp.concatenate([w_in[:, :, :c_rg], w_in[:, :, c_qr:c_gr]], axis=2).astype(BF16),
        "ag": jnp.pad(w_in[:, :, c_ag:c_qr], ((0, 0), (0, 0), (0, GATE_RANK_PAD - rank))).astype(BF16),
        "a2": jnp.pad(w_alpha2, ((0, 0), (0, GATE_RANK_PAD - rank), (0, 0))).astype(BF16),
        "g": jnp.concatenate([w_in[:, :, c_rg:c_ag], w_in[:, :, c_gr:c_mg], w_in[:, :, c_mg:]],
                             axis=2).astype(BF16),
        "oa": w_o_gla.astype(BF16),
        "ob": w_o_ret.astype(BF16),
        "o": w_out.astype(BF16),
    }
    w1u, w1d = w_ffn1_up.astype(BF16), w_ffn1_down.astype(BF16)
    w2u, w2d = w_ffn2_up.astype(BF16), w_ffn2_down.astype(BF16)

    gla_p, ret_p, gla_s, ret_s = [], [], [], []
    for l in range(depth):
        vec = {
            "ba": b_alpha[l][None, :],
            "gna": gn_gla[l][None, :],
            "gnr": gn_ret[l][None, :],
            "bm": b_merge[l][None, :],
            "lng": ln_g[l, 1][None, :],
            "lnb": ln_b[l, 1][None, :],
        }

        def layer(h3, gla0, ret0, tabs, bsz, ns, valid):
            shape = h3.shape
            h = _ffn_ln(h3.reshape(-1, d), w1u, w1d, l, ln_g[l, 0][None, :], ln_b[l, 0][None, :], alpha)
            h, sg, sr = _attn(h.reshape(shape), gla0, ret0, w, l, vec, tabs, bsz, ns, alpha, valid)
            h = _ffn_ln(h.reshape(-1, d), w2u, w2d, l, ln_g[l, 2][None, :], ln_b[l, 2][None, :], alpha)
            return h.reshape(shape), sg, sr

        zero = jnp.zeros((1, N_HEADS, dk, dv), F32)
        hs, sg_s, sr_s = layer(hs, jnp.concatenate([zero, state_gla[l].astype(F32)], axis=0),
                               jnp.concatenate([zero, state_ret[l].astype(F32)], axis=0),
                               tabs_s, sblk, 1, valid_s)
        hp, sg_p, sr_p = layer(hp, sg_s[:1], sr_s[:1], tabs_p, blk, nseq, None)
        gla_p.append(sg_p.astype(x_prompt.dtype))
        ret_p.append(sr_p.astype(x_prompt.dtype))
        gla_s.append(sg_s[1:].astype(state_gla.dtype))
        ret_s.append(sr_s[1:].astype(state_ret.dtype))

    return (hp, hs[1:, :seq_s], jnp.stack(gla_p, axis=0), jnp.stack(ret_p, axis=0),
            jnp.stack(gla_s, axis=0), jnp.stack(ret_s, axis=0))
```

```python
import functools

import numpy as np
import jax
import jax.numpy as jnp
from jax import lax
from jax.experimental import pallas as pl
from jax.experimental.pallas import tpu as pltpu

F32 = jnp.float32
BF16 = jnp.bfloat16

N_HEADS = 4
GATE_RANK_PAD = 128
GATE_TAU = 16.0
PAST_LEN = 1024
ROPE_BASE = 10000.0
LN_EPS = 1e-5
GN_EPS = 1e-5
LOG2E = 1.4426950408889634

SUBLANES = 8
MXU_WIDTH = 256
MAIN_BLOCK = 256
MAIN_SEQS = 1
MAIN_SUBBLOCKS = 2
STAGGER = 26
DIAG_BLOCK = 256
GATE_SLOTS = (1, 1, (1, 0, 1, 0, 1, 0, 1, 0), 0, (1, 1, 0, 0), 0)
SMALL_BLOCK = 32
ROW_TILE = 1024
ROW_SUB = 256
FFN_CHUNK = 6 * MXU_WIDTH
VMEM_LIMIT = 56 * 1024 * 1024

_NT = (((1,), (1,)), ((), ()))
_TN = (((0,), (0,)), ((), ()))


def _dot(a, b):
    return jnp.dot(a, b, preferred_element_type=F32)


def _dot_nt(a, b):
    return lax.dot_general(a, b, _NT, preferred_element_type=F32)


def _dot_tn(a, b):
    return lax.dot_general(a, b, _TN, preferred_element_type=F32)


def _layer_norm(x, g, b):
    mu = jnp.mean(x, axis=-1, keepdims=True)
    xc = x - mu
    var = jnp.mean(xc * xc, axis=-1, keepdims=True)
    return xc * lax.rsqrt(var + LN_EPS) * g + b


def _sigmoid(x):
    return 0.5 + 0.5 * jnp.tanh(0.5 * x)


def _silu(x):
    hx = 0.5 * x
    return hx + hx * jnp.tanh(hx)


def _log_sigmoid(x):
    return jnp.minimum(x, 0.0) - jnp.log1p(jnp.exp(-jnp.abs(x)))


def _resident(shape):
    return pl.BlockSpec(shape, lambda *_: (0,) * len(shape), pipeline_mode=pl.Buffered(1))


def _layer_of(stacked):
    rest = stacked.shape[1:]
    return pl.BlockSpec((None,) + rest, lambda *a: (a[-1][0],) + (0,) * len(rest),
                        pipeline_mode=pl.Buffered(1))


def _row_tile(n_rows, limit):
    t = min(limit, n_rows)
    while n_rows % t or t % SUBLANES:
        t -= SUBLANES
    return t


def _ffn_ln_kernel(_layer_ref, h_ref, wup_ref, wdn_ref, g_ref, b_ref, o_ref, *, d_ff, cuts, sub, alpha):
    n_sub = h_ref.shape[0] // sub
    n_ch = len(cuts) - 1
    items = [(s, j) for s in range(n_sub) for j in range(n_ch)]
    hb = [None] * n_sub
    acc = [None] * n_sub

    def up(s, j):
        if hb[s] is None:
            hb[s] = h_ref[s * sub:(s + 1) * sub, :].astype(BF16)
        return (_dot(hb[s], wup_ref[:, cuts[j]:cuts[j + 1]]),
                _dot(hb[s], wup_ref[:, d_ff + cuts[j]:d_ff + cuts[j + 1]]))

    def finish(s):
        rows = slice(s * sub, (s + 1) * sub)
        o_ref[rows, :] = _layer_norm(alpha * h_ref[rows, :] + 0.5 * acc[s], g_ref[...], b_ref[...])

    nxt = up(*items[0])
    ready = None
    for idx, (s, j) in enumerate(items):
        a, b = nxt
        if idx + 1 < len(items):
            nxt = up(*items[idx + 1])
        if ready is not None:
            finish(ready)
            ready = None
        y = _dot((_silu(a) * b).astype(BF16), wdn_ref[cuts[j]:cuts[j + 1], :])
        acc[s] = y if acc[s] is None else acc[s] + y
        if j == n_ch - 1:
            ready = s
    finish(ready)


def _ffn_ln(h, w_up, w_down, layer, g, b, alpha):
    n, d = h.shape
    d_ff = w_down.shape[1]
    cuts = tuple(range(0, d_ff, FFN_CHUNK)) + (d_ff,)
    tm = _row_tile(n, ROW_TILE)
    sub = ROW_SUB if tm % ROW_SUB == 0 else tm
    return pl.pallas_call(
        functools.partial(_ffn_ln_kernel, d_ff=d_ff, cuts=cuts, sub=sub, alpha=alpha),
        grid_spec=pltpu.PrefetchScalarGridSpec(
            num_scalar_prefetch=1,
            grid=(n // tm,),
            in_specs=[
                pl.BlockSpec((tm, d), lambda i, l: (i, 0)),
                _layer_of(w_up),
                _layer_of(w_down),
                _resident(g.shape),
                _resident(b.shape),
            ],
            out_specs=pl.BlockSpec((tm, d), lambda i, l: (i, 0)),
        ),
        out_shape=jax.ShapeDtypeStruct((n, d), F32),
        compiler_params=pltpu.CompilerParams(
            dimension_semantics=("parallel",), vmem_limit_bytes=VMEM_LIMIT),
        name="ffn_ln",
    )(jnp.asarray([layer], jnp.int32), h, w_up, w_down, g, b)


def _head_norm(o, gain):
    mu = jnp.mean(o, axis=-1, keepdims=True)
    oc = o - mu
    var = jnp.mean(oc * oc, axis=-1, keepdims=True)
    return oc * lax.rsqrt(var + GN_EPS) * gain


def _attn_kernel(_layer_ref, *refs, nseq, nsub, stagger, blk, dblk, dk, dv, masked, alpha):
    n_in = 24 if masked else 23
    (h_ref, wmix_ref, wag_ref, wa2_ref, ba_ref, cos_ref, sin_ref, decq_ref, deck_ref, tri_ref,
     lvl_ref, gla0_ref, ret0_ref, gna_ref, gnr_ref, gpow_ref, wg_ref, bm_ref, woa_ref, wob_ref,
     wo_ref, lng_ref, lnb_ref) = refs[:23]
    valid_ref = refs[23] if masked else None
    o_ref, sg_ref, sr_ref, sgt_scr, sr_scr, b_scr, oa_scr, ob_scr = refs[n_in:]
    hk = N_HEADS * dk
    hv = N_HEADS * dv
    d = h_ref.shape[-1]
    n_diag = blk // dblk
    lv_diag = dblk.bit_length() - 1
    n_lv = blk.bit_length() - 1
    c0 = 2 * hk + hv
    pw = d // 2
    n_gate = 4 * d // pw
    j = pl.program_id(1)

    def shared(ref, s):
        return min(s, ref.shape[0] - 1)

    @pl.when(j == 0)
    def _():
        for s in range(nseq):
            for hd in range(N_HEADS):
                sgt_scr[s, hd] = gla0_ref[shared(gla0_ref, s), hd].T
            sr_scr[s] = ret0_ref[shared(ret0_ref, s)]

    lvl = lvl_ref[...]
    r8 = lax.broadcasted_iota(jnp.int32, (blk, hk), 0) & (SUBLANES - 1)
    causal = (lax.broadcasted_iota(jnp.int32, (blk, blk), 0)
              >= lax.broadcasted_iota(jnp.int32, (blk, blk), 1))

    def tile(x, i, hd):
        return x[i * dblk:(i + 1) * dblk, hd * dk:(hd + 1) * dk]

    def block(s, u):
        rows = slice(u * blk, (u + 1) * blk)
        su = s * nsub + u
        h = h_ref[s, rows, :]
        hb = h.astype(BF16)
        gates = []

        def gate_pieces(n):
            for _ in range(n):
                p = len(gates)
                if p == n_gate:
                    return
                x = _dot(hb, wg_ref[:, p * pw:(p + 1) * pw])
                if p * pw < 2 * d:
                    gates.append(_silu(x))
                else:
                    gates.append(_sigmoid(x + bm_ref[:, p * pw - 2 * d:(p + 1) * pw - 2 * d]))

        def slot(k, i):
            n = GATE_SLOTS[k]
            return n if isinstance(n, int) else (n[i] if i < len(n) else 0)

        ag = _dot(hb, wag_ref[...])
        qg = _dot(hb, wmix_ref[:, 0:hk]) * (dk ** -0.5)
        yield
        gate_in = _dot(ag.astype(BF16), wa2_ref[...])
        kg = _dot(hb, wmix_ref[:, hk:2 * hk])
        vg = _dot(hb, wmix_ref[:, 2 * hk:2 * hk + hv]).astype(BF16)
        gate_pieces(slot(0, 0))
        lg = _log_sigmoid(gate_in + ba_ref[...]) / GATE_TAU
        if masked:
            valid = valid_ref[s, rows, :]
            vmask = jnp.concatenate([valid] * N_HEADS, axis=1)
            lg = lg * vmask
            kg = kg * vmask
        yield
        qr = _dot(hb, wmix_ref[:, c0:c0 + hk])
        kr = _dot(hb, wmix_ref[:, c0 + hk:c0 + 2 * hk])
        lg_hi = lg.astype(BF16)
        rem = lg - lg_hi.astype(F32)
        lg_mid = rem.astype(BF16)
        lg_lo = (rem - lg_mid.astype(F32)).astype(BF16)
        tri = tri_ref[...]
        b2 = (_dot(tri, lg_hi) + _dot(tri, lg_mid) + _dot(tri, lg_lo)) * LOG2E
        b_scr[su] = b2
        vr = _dot(hb, wmix_ref[:, c0 + 2 * hk:c0 + 2 * hk + hv]).astype(BF16)
        gate_pieces(slot(1, 0))
        yield

        def in_group_rows(r):
            return jnp.concatenate(
                [jnp.broadcast_to(b_scr[su, pl.ds(SUBLANES * i + r, 1), :], (SUBLANES, hk))
                 for i in range(blk // SUBLANES)], axis=0)

        def level_factor(lv):
            if lv == 0:
                a = jnp.where(r8 < 2, in_group_rows(0),
                              jnp.where(r8 < 4, in_group_rows(2),
                                        jnp.where(r8 < 6, in_group_rows(4), in_group_rows(6))))
                diff = b2 - a
            elif lv == 1:
                diff = b2 - jnp.where(r8 < 4, in_group_rows(1), in_group_rows(5))
            elif lv == 2:
                diff = b2 - in_group_rows(3)
            else:
                grp = 2 << lv
                diff = jnp.concatenate(
                    [b2[g * grp:(g + 1) * grp] - b_scr[su, pl.ds(g * grp + grp // 2 - 1, 1), :]
                     for g in range(blk // grp)], axis=0)
            return jnp.exp2(-jnp.abs(diff))

        fac = []
        for lv in range(n_lv):
            gate_pieces(slot(2, lv))
            fac.append(level_factor(lv))
            yield
        last = b_scr[su, pl.ds(blk - 1, 1), :]
        q_in = (qg * jnp.exp2(b2)).astype(BF16)
        k_out = (kg * jnp.exp2(-jnp.abs(last - b2))).astype(BF16)
        s_dec = jnp.exp2(last)
        yield

        qgb = qg.astype(BF16)
        kgb = kg.astype(BF16)
        on_diag = lvl == lv_diag
        att = [[jnp.where(on_diag, _dot_nt(tile(qgb, i, hd), tile(kgb, i, hd)), 0.0)
                for hd in range(N_HEADS)] for i in range(n_diag)]
        yield
        for lv in range(lv_diag):
            qf = (qg * fac[lv]).astype(BF16)
            kf = (kg * fac[lv]).astype(BF16)
            at_level = lvl == lv
            for i in range(n_diag):
                for hd in range(N_HEADS):
                    att[i][hd] = jnp.where(at_level, _dot_nt(tile(qf, i, hd), tile(kf, i, hd)), att[i][hd])
            gate_pieces(slot(3, lv))
            yield
        if n_diag == 2:
            qf = (qg[dblk:] * fac[lv_diag][dblk:]).astype(BF16)
            kf = (kg[:dblk] * fac[lv_diag][:dblk]).astype(BF16)
        gna = gna_ref[...]
        for hd in range(N_HEADS):
            ks = slice(hd * dk, (hd + 1) * dk)
            vs = slice(hd * dv, (hd + 1) * dv)
            v = vg[:, vs]
            if n_diag == 2:
                below = _dot_nt(qf[:, ks], kf[:, ks])
                o = jnp.concatenate([
                    _dot(att[0][hd].astype(BF16), v[:dblk]),
                    _dot(jnp.concatenate([below, att[1][hd]], axis=1).astype(BF16), v)], axis=0)
            else:
                o = _dot(att[0][hd].astype(BF16), v)
            st = sgt_scr[s, hd]
            o = o + _dot_nt(q_in[:, ks], st.astype(BF16))
            sgt_scr[s, hd] = st * s_dec[:, ks] + _dot_tn(v, k_out[:, ks])
            oa_scr[su, :, vs] = _head_norm(o, gna[:, vs])
            gate_pieces(slot(4, hd))
            yield

        cos = cos_ref[shared(cos_ref, s), rows, :]
        sin = sin_ref[shared(sin_ref, s), rows, :]
        decq = decq_ref[...]
        deck = deck_ref[...]
        gpow = gpow_ref[shared(gpow_ref, s)]
        gnr = gnr_ref[...]
        for hd in range(N_HEADS):
            ks = slice(hd * dk, (hd + 1) * dk)
            vs = slice(hd * dv, (hd + 1) * dv)
            q = qr[:, ks]
            k = kr[:, ks]
            q = (q * cos + pltpu.roll(q, dk // 2, 1) * sin) * decq[:, ks]
            k = (k * cos + pltpu.roll(k, dk // 2, 1) * sin) * deck[:, ks]
            if masked:
                k = k * valid
            qb = q.astype(BF16)
            kb = k.astype(BF16)
            v = vr[:, vs]
            sc = jnp.where(causal, _dot_nt(qb, kb), 0.0)
            s0 = sr_scr[s, hd]
            o = _dot(sc.astype(BF16), v) + _dot(qb, s0.astype(BF16))
            sr_scr[s, hd] = (s0 + _dot_tn(kb, v)) * gpow[:, vs]
            ob_scr[su, :, vs] = _head_norm(o, gnr[:, vs])
            gate_pieces(slot(5, hd))
            yield

        gate_pieces(n_gate)
        rg, gr, ga, gb = [jnp.concatenate(gates[i * (d // pw):(i + 1) * (d // pw)], axis=1)
                          for i in range(4)]
        ya = _dot((rg * oa_scr[su]).astype(BF16), woa_ref[...])
        yb = _dot((gr * ob_scr[su]).astype(BF16), wob_ref[...])
        y = _dot((ga * ya + gb * yb).astype(BF16), wo_ref[...])
        o_ref[s, rows, :] = _layer_norm(alpha * h + y, lng_ref[...], lnb_ref[...])
        yield

    waiting = [block(s, u) for u in range(nsub) for s in range(nseq)]
    running = []
    tick = 0
    while waiting or running:
        if waiting and tick % stagger == 0:
            running.append(waiting.pop(0))
        for g in list(running):
            try:
                next(g)
            except StopIteration:
                running.remove(g)
        tick += 1

    @pl.when(j == pl.num_programs(1) - 1)
    def _():
        for s in range(nseq):
            for hd in range(N_HEADS):
                sg_ref[s, hd] = sgt_scr[s, hd].T
            sr_ref[s] = sr_scr[s]


def _level_matrix(n):
    r = np.arange(n)[:, None]
    c = np.arange(n)[None, :]
    x = np.maximum(r ^ c, 1)
    lvl = np.where(r > c, np.floor(np.log2(x)), np.where(r == c, n.bit_length() - 1, -1))
    return lvl.astype(np.int32)


def _attn(h, gla0, ret0, w, layer, vec, tabs, blk, nseq, alpha, valid=None):
    nb, t, d = h.shape
    _, _, dk, dv = gla0.shape
    hk, hv = N_HEADS * dk, N_HEADS * dv
    dblk = min(blk, DIAG_BLOCK)
    nsub = MAIN_SUBBLOCKS if (valid is None and t % (MAIN_SUBBLOCKS * blk) == 0) else 1
    tstep = nsub * blk
    assert t % tstep == 0 and blk & (blk - 1) == 0 and SUBLANES <= blk <= 256 and blk // dblk <= 2
    assert nb % nseq == 0
    lvl = jnp.asarray(_level_matrix(dblk))
    tri = jnp.asarray(np.tril(np.ones((blk, blk), np.float32)), BF16)
    masked = valid is not None

    def per_seq(a, width):
        if a.shape[0] > 1:
            return pl.BlockSpec((nseq, tstep, width), lambda b, j, l: (b, j, 0))
        return pl.BlockSpec((1, tstep, width), lambda b, j, l: (0, j, 0))

    def state_in(a):
        if a.shape[0] > 1:
            return pl.BlockSpec((nseq, N_HEADS, dk, dv), lambda b, j, l: (b, 0, 0, 0))
        return pl.BlockSpec((1, N_HEADS, dk, dv), lambda b, j, l: (0, 0, 0, 0))

    seq = pl.BlockSpec((nseq, tstep, d), lambda b, j, l: (b, j, 0))
    args = [h, w["mix"], w["ag"], w["a2"], vec["ba"], tabs["cos"], tabs["sin"], tabs["decq"], tabs["deck"],
            tri, lvl, gla0, ret0, vec["gna"], vec["gnr"], tabs["gpow"], w["g"], vec["bm"], w["oa"], w["ob"],
            w["o"], vec["lng"], vec["lnb"]]
    in_specs = [
        seq,
        _layer_of(w["mix"]), _layer_of(w["ag"]), _layer_of(w["a2"]),
        _resident(vec["ba"].shape),
        per_seq(tabs["cos"], dk), per_seq(tabs["sin"], dk),
        _resident(tabs["decq"].shape), _resident(tabs["deck"].shape),
        _resident(tri.shape), _resident(lvl.shape),
        state_in(gla0), state_in(ret0),
        _resident(vec["gna"].shape), _resident(vec["gnr"].shape),
        (pl.BlockSpec((nseq, 1, hv), lambda b, j, l: (b, 0, 0)) if tabs["gpow"].shape[0] > 1
         else pl.BlockSpec((1, 1, hv), lambda b, j, l: (0, 0, 0))),
        _layer_of(w["g"]), _resident(vec["bm"].shape), _layer_of(w["oa"]),
        _layer_of(w["ob"]), _layer_of(w["o"]), _resident(vec["lng"].shape),
        _resident(vec["lnb"].shape),
    ]
    if masked:
        in_specs.append(pl.BlockSpec((nseq, tstep, dk), lambda b, j, l: (b, j, 0)))
        args.append(valid)
    state_out = pl.BlockSpec((nseq, N_HEADS, dk, dv), lambda b, j, l: (b, 0, 0, 0))
    return pl.pallas_call(
        functools.partial(_attn_kernel, nseq=nseq, nsub=nsub, stagger=STAGGER, blk=blk, dblk=dblk, dk=dk, dv=dv,
                          masked=masked, alpha=alpha),
        grid_spec=pltpu.PrefetchScalarGridSpec(
            num_scalar_prefetch=1,
            grid=(nb // nseq, t // tstep),
            in_specs=in_specs,
            out_specs=[seq, state_out, state_out],
            scratch_shapes=[
                pltpu.VMEM((nseq, N_HEADS, dv, dk), F32), pltpu.VMEM((nseq, N_HEADS, dk, dv), F32),
                pltpu.VMEM((nseq * nsub, blk, hk), F32), pltpu.VMEM((nseq * nsub, blk, hv), F32),
                pltpu.VMEM((nseq * nsub, blk, hv), F32)],
        ),
        out_shape=[jax.ShapeDtypeStruct((nb, t, d), F32),
                   jax.ShapeDtypeStruct((nb, N_HEADS, dk, dv), F32),
                   jax.ShapeDtypeStruct((nb, N_HEADS, dk, dv), F32)],
        compiler_params=pltpu.CompilerParams(
            dimension_semantics=("parallel", "arbitrary"), vmem_limit_bytes=VMEM_LIMIT),
        name="attn",
    )(jnp.asarray([layer], jnp.int32), *args)


def _mix_tables(pos, blk, n_valid, dk, dv):
    half = dk // 2
    inv = ROPE_BASE ** (-jnp.arange(half, dtype=F32) / half)
    ang = pos.astype(F32)[..., None] * inv
    cos = jnp.cos(ang)
    sin = jnp.sin(ang)
    log_gamma = jnp.log1p(-(2.0 ** (-5.0 - jnp.arange(N_HEADS, dtype=F32))))
    step = jnp.arange(blk, dtype=F32)[:, None, None] + 1.0
    dec = jnp.broadcast_to(step * log_gamma[None, :, None], (blk, N_HEADS, dk)).reshape(blk, N_HEADS * dk)
    gpow = jnp.exp(n_valid.astype(F32)[:, None, None] * log_gamma[None, :, None])
    gpow = jnp.broadcast_to(gpow, (pos.shape[0], N_HEADS, dv)).reshape(pos.shape[0], 1, N_HEADS * dv)
    return {
        "cos": jnp.concatenate([cos, cos], axis=-1),
        "sin": jnp.concatenate([-sin, sin], axis=-1),
        "decq": jnp.exp(dec),
        "deck": jnp.exp(-dec) * dk ** -0.5,
        "gpow": gpow,
    }


def kernel(x_prompt, x_sample, state_gla, state_ret, meta, ln_g, ln_b, w_ffn1_up, w_ffn1_down, w_in,
           w_alpha2, b_alpha, b_merge, gn_gla, gn_ret, w_o_gla, w_o_ret, w_out, w_ffn2_up, w_ffn2_down):
    nb, seq, d = x_prompt.shape
    nbs, seq_s, _ = x_sample.shape
    depth = ln_g.shape[0]
    n_meta = meta.shape[0]
    _, _, _, dk, dv = state_gla.shape
    hk, hv = N_HEADS * dk, N_HEADS * dv
    rank = w_alpha2.shape[1]
    alpha = (2.0 * depth) ** 0.25
    blk = min(MAIN_BLOCK, seq)
    sblk = SMALL_BLOCK
    nseq = MAIN_SEQS if nb % MAIN_SEQS == 0 else 1
    assert n_meta <= sblk and seq_s <= sblk

    hs = jnp.zeros((1 + nbs, sblk, d), F32)
    hs = hs.at[0, :n_meta].set(meta.astype(F32)).at[1:, :seq_s].set(x_sample)
    n_valid_s = jnp.asarray([n_meta] + [seq_s] * nbs, jnp.int32)
    valid_s = (jnp.arange(sblk)[None, :, None] < n_valid_s[:, None, None]).astype(F32)
    valid_s = jnp.broadcast_to(valid_s, (1 + nbs, sblk, dk))
    off = jnp.asarray([0] + [n_meta + PAST_LEN] * nbs, jnp.int32)
    tabs_s = _mix_tables(off[:, None] + jnp.arange(sblk)[None, :], sblk, n_valid_s, dk, dv)
    tabs_p = _mix_tables(n_meta + jnp.arange(seq)[None, :], blk, jnp.asarray([blk], jnp.int32), dk, dv)
    hp = x_prompt

    c_rg = 2 * hk + hv
    c_ag = c_rg + hv
    c_qr = c_ag + rank
    c_gr = c_qr + 2 * hk + hv
    c_mg = c_gr + hv
    w = {
        "mix": jnp.concatenate([w_in[:, :, :c_rg], w_in[:, :, c_qr:c_gr]], axis=2).astype(BF16),
        "ag": jnp.pad(w_in[:, :, c_ag:c_qr], ((0, 0), (0, 0), (0, GATE_RANK_PAD - rank))).astype(BF16),
        "a2": jnp.pad(w_alpha2, ((0, 0), (0, GATE_RANK_PAD - rank), (0, 0))).astype(BF16),
        "g": jnp.concatenate([w_in[:, :, c_rg:c_ag], w_in[:, :, c_gr:c_mg], w_in[:, :, c_mg:]],
                             axis=2).astype(BF16),
        "oa": w_o_gla.astype(BF16),
        "ob": w_o_ret.astype(BF16),
        "o": w_out.astype(BF16),
    }
    w1u, w1d = w_ffn1_up.astype(BF16), w_ffn1_down.astype(BF16)
    w2u, w2d = w_ffn2_up.astype(BF16), w_ffn2_down.astype(BF16)

    gla_p, ret_p, gla_s, ret_s = [], [], [], []
    for l in range(depth):
        vec = {
            "ba": b_alpha[l][None, :],
            "gna": gn_gla[l][None, :],
            "gnr": gn_ret[l][None, :],
            "bm": b_merge[l][None, :],
            "lng": ln_g[l, 1][None, :],
            "lnb": ln_b[l, 1][None, :],
        }

        def layer(h3, gla0, ret0, tabs, bsz, ns, valid):
            shape = h3.shape
            h = _ffn_ln(h3.reshape(-1, d), w1u, w1d, l, ln_g[l, 0][None, :], ln_b[l, 0][None, :], alpha)
            h, sg, sr = _attn(h.reshape(shape), gla0, ret0, w, l, vec, tabs, bsz, ns, alpha, valid)
            h = _ffn_ln(h.reshape(-1, d), w2u, w2d, l, ln_g[l, 2][None, :], ln_b[l, 2][None, :], alpha)
            return h.reshape(shape), sg, sr

        zero = jnp.zeros((1, N_HEADS, dk, dv), F32)
        hs, sg_s, sr_s = layer(hs, jnp.concatenate([zero, state_gla[l].astype(F32)], axis=0),
                               jnp.concatenate([zero, state_ret[l].astype(F32)], axis=0),
                               tabs_s, sblk, 1, valid_s)
        hp, sg_p, sr_p = layer(hp, sg_s[:1], sr_s[:1], tabs_p, blk, nseq, None)
        gla_p.append(sg_p.astype(x_prompt.dtype))
        ret_p.append(sr_p.astype(x_prompt.dtype))
        gla_s.append(sg_s[1:].astype(state_gla.dtype))
        ret_s.append(sr_s[1:].astype(state_ret.dtype))

    return (hp, hs[1:, :seq_s], jnp.stack(gla_p, axis=0), jnp.stack(ret_p, axis=0),
            jnp.stack(gla_s, axis=0), jnp.stack(ret_s, axis=0))
```

```python
import functools

import numpy as np
import jax
import jax.numpy as jnp
from jax import lax
from jax.experimental import pallas as pl
from jax.experimental.pallas import tpu as pltpu

F32 = jnp.float32
BF16 = jnp.bfloat16

N_HEADS = 4
GATE_RANK_PAD = 128
GATE_TAU = 16.0
PAST_LEN = 1024
ROPE_BASE = 10000.0
LN_EPS = 1e-5
GN_EPS = 1e-5
LOG2E = 1.4426950408889634

SUBLANES = 8
MXU_WIDTH = 256
MAIN_BLOCK = 256
MAIN_SEQS = 1
MAIN_SUBBLOCKS = 2
STAGGER = 29
DIAG_BLOCK = 128
GATE_SLOTS = (1, 1, (1, 0, 1, 0, 1, 0, 1, 0), 0, (1, 1, 0, 0), 0)
SMALL_BLOCK = 32
ROW_TILE = 1024
ROW_SUB = 256
FFN_CHUNK = 6 * MXU_WIDTH
VMEM_LIMIT = 56 * 1024 * 1024

_NT = (((1,), (1,)), ((), ()))
_TN = (((0,), (0,)), ((), ()))


def _dot(a, b):
    return jnp.dot(a, b, preferred_element_type=F32)


def _dot_nt(a, b):
    return lax.dot_general(a, b, _NT, preferred_element_type=F32)


def _dot_tn(a, b):
    return lax.dot_general(a, b, _TN, preferred_element_type=F32)


def _layer_norm(x, g, b):
    mu = jnp.mean(x, axis=-1, keepdims=True)
    xc = x - mu
    var = jnp.mean(xc * xc, axis=-1, keepdims=True)
    return xc * lax.rsqrt(var + LN_EPS) * g + b


def _sigmoid(x):
    return 0.5 + 0.5 * jnp.tanh(0.5 * x)


def _silu(x):
    hx = 0.5 * x
    return hx + hx * jnp.tanh(hx)


def _log_sigmoid(x):
    return jnp.minimum(x, 0.0) - jnp.log1p(jnp.exp(-jnp.abs(x)))


def _resident(shape):
    return pl.BlockSpec(shape, lambda *_: (0,) * len(shape), pipeline_mode=pl.Buffered(1))


def _layer_of(stacked):
    rest = stacked.shape[1:]
    return pl.BlockSpec((None,) + rest, lambda *a: (a[-1][0],) + (0,) * len(rest),
                        pipeline_mode=pl.Buffered(1))


def _row_tile(n_rows, limit):
    t = min(limit, n_rows)
    while n_rows % t or t % SUBLANES:
        t -= SUBLANES
    return t


def _ffn_ln_kernel(_layer_ref, h_ref, wup_ref, wdn_ref, g_ref, b_ref, o_ref, *, d_ff, cuts, sub, alpha):
    n_sub = h_ref.shape[0] // sub
    n_ch = len(cuts) - 1
    items = [(s, j) for s in range(n_sub) for j in range(n_ch)]
    hb = [None] * n_sub
    acc = [None] * n_sub

    def up(s, j):
        if hb[s] is None:
            hb[s] = h_ref[s * sub:(s + 1) * sub, :].astype(BF16)
        return (_dot(hb[s], wup_ref[:, cuts[j]:cuts[j + 1]]),
                _dot(hb[s], wup_ref[:, d_ff + cuts[j]:d_ff + cuts[j + 1]]))

    def finish(s):
        rows = slice(s * sub, (s + 1) * sub)
        o_ref[rows, :] = _layer_norm(alpha * h_ref[rows, :] + 0.5 * acc[s], g_ref[...], b_ref[...])

    nxt = up(*items[0])
    ready = None
    for idx, (s, j) in enumerate(items):
        a, b = nxt
        if idx + 1 < len(items):
            nxt = up(*items[idx + 1])
        if ready is not None:
            finish(ready)
            ready = None
        y = _dot((_silu(a) * b).astype(BF16), wdn_ref[cuts[j]:cuts[j + 1], :])
        acc[s] = y if acc[s] is None else acc[s] + y
        if j == n_ch - 1:
            ready = s
    finish(ready)


def _ffn_ln(h, w_up, w_down, layer, g, b, alpha):
    n, d = h.shape
    d_ff = w_down.shape[1]
    cuts = tuple(range(0, d_ff, FFN_CHUNK)) + (d_ff,)
    tm = _row_tile(n, ROW_TILE)
    sub = ROW_SUB if tm % ROW_SUB == 0 else tm
    return pl.pallas_call(
        functools.partial(_ffn_ln_kernel, d_ff=d_ff, cuts=cuts, sub=sub, alpha=alpha),
        grid_spec=pltpu.PrefetchScalarGridSpec(
            num_scalar_prefetch=1,
            grid=(n // tm,),
            in_specs=[
                pl.BlockSpec((tm, d), lambda i, l: (i, 0)),
                _layer_of(w_up),
                _layer_of(w_down),
                _resident(g.shape),
                _resident(b.shape),
            ],
            out_specs=pl.BlockSpec((tm, d), lambda i, l: (i, 0)),
        ),
        out_shape=jax.ShapeDtypeStruct((n, d), F32),
        compiler_params=pltpu.CompilerParams(
            dimension_semantics=("parallel",), vmem_limit_bytes=VMEM_LIMIT),
        name="ffn_ln",
    )(jnp.asarray([layer], jnp.int32), h, w_up, w_down, g, b)


def _head_norm(o, gain):
    mu = jnp.mean(o, axis=-1, keepdims=True)
    oc = o - mu
    var = jnp.mean(oc * oc, axis=-1, keepdims=True)
    return oc * lax.rsqrt(var + GN_EPS) * gain


def _attn_kernel(_layer_ref, *refs, nseq, nsub, stagger, blk, dblk, dk, dv, masked, alpha):
    n_in = 24 if masked else 23
    (h_ref, wmix_ref, wag_ref, wa2_ref, ba_ref, cos_ref, sin_ref, decq_ref, deck_ref, tri_ref,
     lvl_ref, gla0_ref, ret0_ref, gna_ref, gnr_ref, gpow_ref, wg_ref, bm_ref, woa_ref, wob_ref,
     wo_ref, lng_ref, lnb_ref) = refs[:23]
    valid_ref = refs[23] if masked else None
    o_ref, sg_ref, sr_ref, sgt_scr, sr_scr, b_scr, oa_scr, ob_scr = refs[n_in:]
    hk = N_HEADS * dk
    hv = N_HEADS * dv
    d = h_ref.shape[-1]
    n_diag = blk // dblk
    lv_diag = dblk.bit_length() - 1
    n_lv = blk.bit_length() - 1
    c0 = 2 * hk + hv
    pw = d // 2
    n_gate = 4 * d // pw
    j = pl.program_id(1)

    def shared(ref, s):
        return min(s, ref.shape[0] - 1)

    @pl.when(j == 0)
    def _():
        for s in range(nseq):
            for hd in range(N_HEADS):
                sgt_scr[s, hd] = gla0_ref[shared(gla0_ref, s), hd].T
            sr_scr[s] = ret0_ref[shared(ret0_ref, s)]

    lvl = lvl_ref[...]
    r8 = lax.broadcasted_iota(jnp.int32, (blk, hk), 0) & (SUBLANES - 1)
    causal = (lax.broadcasted_iota(jnp.int32, (blk, blk), 0)
              >= lax.broadcasted_iota(jnp.int32, (blk, blk), 1))

    def tile(x, i, hd):
        return x[i * dblk:(i + 1) * dblk, hd * dk:(hd + 1) * dk]

    def block(s, u):
        rows = slice(u * blk, (u + 1) * blk)
        su = s * nsub + u
        h = h_ref[s, rows, :]
        hb = h.astype(BF16)
        gates = []

        def gate_pieces(n):
            for _ in range(n):
                p = len(gates)
                if p == n_gate:
                    return
                x = _dot(hb, wg_ref[:, p * pw:(p + 1) * pw])
                if p * pw < 2 * d:
                    gates.append(_silu(x))
                else:
                    gates.append(_sigmoid(x + bm_ref[:, p * pw - 2 * d:(p + 1) * pw - 2 * d]))

        def slot(k, i):
            n = GATE_SLOTS[k]
            return n if isinstance(n, int) else (n[i] if i < len(n) else 0)

        ag = _dot(hb, wag_ref[...])
        qg = _dot(hb, wmix_ref[:, 0:hk]) * (dk ** -0.5)
        yield
        gate_in = _dot(ag.astype(BF16), wa2_ref[...])
        kg = _dot(hb, wmix_ref[:, hk:2 * hk])
        vg = _dot(hb, wmix_ref[:, 2 * hk:2 * hk + hv]).astype(BF16)
        gate_pieces(slot(0, 0))
        lg = _log_sigmoid(gate_in + ba_ref[...]) / GATE_TAU
        if masked:
            valid = valid_ref[s, rows, :]
            vmask = jnp.concatenate([valid] * N_HEADS, axis=1)
            lg = lg * vmask
            kg = kg * vmask
        yield
        qr = _dot(hb, wmix_ref[:, c0:c0 + hk])
        kr = _dot(hb, wmix_ref[:, c0 + hk:c0 + 2 * hk])
        lg_hi = lg.astype(BF16)
        rem = lg - lg_hi.astype(F32)
        lg_mid = rem.astype(BF16)
        lg_lo = (rem - lg_mid.astype(F32)).astype(BF16)
        tri = tri_ref[...]
        b2 = (_dot(tri, lg_hi) + _dot(tri, lg_mid) + _dot(tri, lg_lo)) * LOG2E
        b_scr[su] = b2
        vr = _dot(hb, wmix_ref[:, c0 + 2 * hk:c0 + 2 * hk + hv]).astype(BF16)
        gate_pieces(slot(1, 0))
        yield

        def in_group_rows(r):
            return jnp.concatenate(
                [jnp.broadcast_to(b_scr[su, pl.ds(SUBLANES * i + r, 1), :], (SUBLANES, hk))
                 for i in range(blk // SUBLANES)], axis=0)

        def level_factor(lv):
            if lv == 0:
                a = jnp.where(r8 < 2, in_group_rows(0),
                              jnp.where(r8 < 4, in_group_rows(2),
                                        jnp.where(r8 < 6, in_group_rows(4), in_group_rows(6))))
                diff = b2 - a
            elif lv == 1:
                diff = b2 - jnp.where(r8 < 4, in_group_rows(1), in_group_rows(5))
            elif lv == 2:
                diff = b2 - in_group_rows(3)
            else:
                grp = 2 << lv
                diff = jnp.concatenate(
                    [b2[g * grp:(g + 1) * grp] - b_scr[su, pl.ds(g * grp + grp // 2 - 1, 1), :]
                     for g in range(blk // grp)], axis=0)
            return jnp.exp2(-jnp.abs(diff))

        fac = []
        for lv in range(n_lv):
            gate_pieces(slot(2, lv))
            fac.append(level_factor(lv))
            yield
        last = b_scr[su, pl.ds(blk - 1, 1), :]
        q_in = (qg * jnp.exp2(b2)).astype(BF16)
        k_out = (kg * jnp.exp2(-jnp.abs(last - b2))).astype(BF16)
        s_dec = jnp.exp2(last)
        yield

        qgb = qg.astype(BF16)
        kgb = kg.astype(BF16)
        on_diag = lvl == lv_diag
        att = [[jnp.where(on_diag, _dot_nt(tile(qgb, i, hd), tile(kgb, i, hd)), 0.0)
                for hd in range(N_HEADS)] for i in range(n_diag)]
        yield
        for lv in range(lv_diag):
            qf = (qg * fac[lv]).astype(BF16)
            kf = (kg * fac[lv]).astype(BF16)
            at_level = lvl == lv
            for i in range(n_diag):
                for hd in range(N_HEADS):
                    att[i][hd] = jnp.where(at_level, _dot_nt(tile(qf, i, hd), tile(kf, i, hd)), att[i][hd])
            gate_pieces(slot(3, lv))
            yield
        if n_diag == 2:
            qf = (qg[dblk:] * fac[lv_diag][dblk:]).astype(BF16)
            kf = (kg[:dblk] * fac[lv_diag][:dblk]).astype(BF16)
        gna = gna_ref[...]
        for hd in range(N_HEADS):
            ks = slice(hd * dk, (hd + 1) * dk)
            vs = slice(hd * dv, (hd + 1) * dv)
            v = vg[:, vs]
            if n_diag == 2:
                below = _dot_nt(qf[:, ks], kf[:, ks])
                o = jnp.concatenate([
                    _dot(att[0][hd].astype(BF16), v[:dblk]),
                    _dot(jnp.concatenate([below, att[1][hd]], axis=1).astype(BF16), v)], axis=0)
            else:
                o = _dot(att[0][hd].astype(BF16), v)
            st = sgt_scr[s, hd]
            o = o + _dot_nt(q_in[:, ks], st.astype(BF16))
            sgt_scr[s, hd] = st * s_dec[:, ks] + _dot_tn(v, k_out[:, ks])
            oa_scr[su, :, vs] = _head_norm(o, gna[:, vs])
            gate_pieces(slot(4, hd))
            yield

        cos = cos_ref[shared(cos_ref, s), rows, :]
        sin = sin_ref[shared(sin_ref, s), rows, :]
        decq = decq_ref[...]
        deck = deck_ref[...]
        gpow = gpow_ref[shared(gpow_ref, s)]
        gnr = gnr_ref[...]
        for hd in range(N_HEADS):
            ks = slice(hd * dk, (hd + 1) * dk)
            vs = slice(hd * dv, (hd + 1) * dv)
            q = qr[:, ks]
            k = kr[:, ks]
            q = (q * cos + pltpu.roll(q, dk // 2, 1) * sin) * decq[:, ks]
            k = (k * cos + pltpu.roll(k, dk // 2, 1) * sin) * deck[:, ks]
            if masked:
                k = k * valid
            qb = q.astype(BF16)
            kb = k.astype(BF16)
            v = vr[:, vs]
            sc = jnp.where(causal, _dot_nt(qb, kb), 0.0)
            s0 = sr_scr[s, hd]
            o = _dot(sc.astype(BF16), v) + _dot(qb, s0.astype(BF16))
            sr_scr[s, hd] = (s0 + _dot_tn(kb, v)) * gpow[:, vs]
            ob_scr[su, :, vs] = _head_norm(o, gnr[:, vs])
            gate_pieces(slot(5, hd))
            yield

        gate_pieces(n_gate)
        rg, gr, ga, gb = [jnp.concatenate(gates[i * (d // pw):(i + 1) * (d // pw)], axis=1)
                          for i in range(4)]
        ya = _dot((rg * oa_scr[su]).astype(BF16), woa_ref[...])
        yb = _dot((gr * ob_scr[su]).astype(BF16), wob_ref[...])
        y = _dot((ga * ya + gb * yb).astype(BF16), wo_ref[...])
        o_ref[s, rows, :] = _layer_norm(alpha * h + y, lng_ref[...], lnb_ref[...])
        yield

    waiting = [block(s, u) for u in range(nsub) for s in range(nseq)]
    running = []
    tick = 0
    while waiting or running:
        if waiting and tick % stagger == 0:
            running.append(waiting.pop(0))
        for g in list(running):
            try:
                next(g)
            except StopIteration:
                running.remove(g)
        tick += 1

    @pl.when(j == pl.num_programs(1) - 1)
    def _():
        for s in range(nseq):
            for hd in range(N_HEADS):
                sg_ref[s, hd] = sgt_scr[s, hd].T
            sr_ref[s] = sr_scr[s]


def _level_matrix(n):
    r = np.arange(n)[:, None]
    c = np.arange(n)[None, :]
    x = np.maximum(r ^ c, 1)
    lvl = np.where(r > c, np.floor(np.log2(x)), np.where(r == c, n.bit_length() - 1, -1))
    return lvl.astype(np.int32)


def _attn(h, gla0, ret0, w, layer, vec, tabs, blk, nseq, alpha, valid=None):
    nb, t, d = h.shape
    _, _, dk, dv = gla0.shape
    hk, hv = N_HEADS * dk, N_HEADS * dv
    dblk = min(blk, DIAG_BLOCK)
    nsub = MAIN_SUBBLOCKS if (valid is None and t % (MAIN_SUBBLOCKS * blk) == 0) else 1
    tstep = nsub * blk
    assert t % tstep == 0 and blk & (blk - 1) == 0 and SUBLANES <= blk <= 256 and blk // dblk <= 2
    assert nb % nseq == 0
    lvl = jnp.asarray(_level_matrix(dblk))
    tri = jnp.asarray(np.tril(np.ones((blk, blk), np.float32)), BF16)
    masked = valid is not None

    def per_seq(a, width):
        if a.shape[0] > 1:
            return pl.BlockSpec((nseq, tstep, width), lambda b, j, l: (b, j, 0))
        return pl.BlockSpec((1, tstep, width), lambda b, j, l: (0, j, 0))

    def state_in(a):
        if a.shape[0] > 1:
            return pl.BlockSpec((nseq, N_HEADS, dk, dv), lambda b, j, l: (b, 0, 0, 0))
        return pl.BlockSpec((1, N_HEADS, dk, dv), lambda b, j, l: (0, 0, 0, 0))

    seq = pl.BlockSpec((nseq, tstep, d), lambda b, j, l: (b, j, 0))
    args = [h, w["mix"], w["ag"], w["a2"], vec["ba"], tabs["cos"], tabs["sin"], tabs["decq"], tabs["deck"],
            tri, lvl, gla0, ret0, vec["gna"], vec["gnr"], tabs["gpow"], w["g"], vec["bm"], w["oa"], w["ob"],
            w["o"], vec["lng"], vec["lnb"]]
    in_specs = [
        seq,
        _layer_of(w["mix"]), _layer_of(w["ag"]), _layer_of(w["a2"]),
        _resident(vec["ba"].shape),
        per_seq(tabs["cos"], dk), per_seq(tabs["sin"], dk),
        _resident(tabs["decq"].shape), _resident(tabs["deck"].shape),
        _resident(tri.shape), _resident(lvl.shape),
        state_in(gla0), state_in(ret0),
        _resident(vec["gna"].shape), _resident(vec["gnr"].shape),
        (pl.BlockSpec((nseq, 1, hv), lambda b, j, l: (b, 0, 0)) if tabs["gpow"].shape[0] > 1
         else pl.BlockSpec((1, 1, hv), lambda b, j, l: (0, 0, 0))),
        _layer_of(w["g"]), _resident(vec["bm"].shape), _layer_of(w["oa"]),
        _layer_of(w["ob"]), _layer_of(w["o"]), _resident(vec["lng"].shape),
        _resident(vec["lnb"].shape),
    ]
    if masked:
        in_specs.append(pl.BlockSpec((nseq, tstep, dk), lambda b, j, l: (b, j, 0)))
        args.append(valid)
    state_out = pl.BlockSpec((nseq, N_HEADS, dk, dv), lambda b, j, l: (b, 0, 0, 0))
    return pl.pallas_call(
        functools.partial(_attn_kernel, nseq=nseq, nsub=nsub, stagger=STAGGER, blk=blk, dblk=dblk, dk=dk, dv=dv,
                          masked=masked, alpha=alpha),
        grid_spec=pltpu.PrefetchScalarGridSpec(
            num_scalar_prefetch=1,
            grid=(nb // nseq, t // tstep),
            in_specs=in_specs,
            out_specs=[seq, state_out, state_out],
            scratch_shapes=[
                pltpu.VMEM((nseq, N_HEADS, dv, dk), F32), pltpu.VMEM((nseq, N_HEADS, dk, dv), F32),
                pltpu.VMEM((nseq * nsub, blk, hk), F32), pltpu.VMEM((nseq * nsub, blk, hv), F32),
                pltpu.VMEM((nseq * nsub, blk, hv), F32)],
        ),
        out_shape=[jax.ShapeDtypeStruct((nb, t, d), F32),
                   jax.ShapeDtypeStruct((nb, N_HEADS, dk, dv), F32),
                   jax.ShapeDtypeStruct((nb, N_HEADS, dk, dv), F32)],
        compiler_params=pltpu.CompilerParams(
            dimension_semantics=("parallel", "arbitrary"), vmem_limit_bytes=VMEM_LIMIT),
        name="attn",
    )(jnp.asarray([layer], jnp.int32), *args)


def _mix_tables(pos, blk, n_valid, dk, dv):
    half = dk // 2
    inv = ROPE_BASE ** (-jnp.arange(half, dtype=F32) / half)
    ang = pos.astype(F32)[..., None] * inv
    cos = jnp.cos(ang)
    sin = jnp.sin(ang)
    log_gamma = jnp.log1p(-(2.0 ** (-5.0 - jnp.arange(N_HEADS, dtype=F32))))
    step = jnp.arange(blk, dtype=F32)[:, None, None] + 1.0
    dec = jnp.broadcast_to(step * log_gamma[None, :, None], (blk, N_HEADS, dk)).reshape(blk, N_HEADS * dk)
    gpow = jnp.exp(n_valid.astype(F32)[:, None, None] * log_gamma[None, :, None])
    gpow = jnp.broadcast_to(gpow, (pos.shape[0], N_HEADS, dv)).reshape(pos.shape[0], 1, N_HEADS * dv)
    return {
        "cos": jnp.concatenate([cos, cos], axis=-1),
        "sin": jnp.concatenate([-sin, sin], axis=-1),
        "decq": jnp.exp(dec),
        "deck": jnp.exp(-dec) * dk ** -0.5,
        "gpow": gpow,
    }


def kernel(x_prompt, x_sample, state_gla, state_ret, meta, ln_g, ln_b, w_ffn1_up, w_ffn1_down, w_in,
           w_alpha2, b_alpha, b_merge, gn_gla, gn_ret, w_o_gla, w_o_ret, w_out, w_ffn2_up, w_ffn2_down):
    nb, seq, d = x_prompt.shape
    nbs, seq_s, _ = x_sample.shape
    depth = ln_g.shape[0]
    n_meta = meta.shape[0]
    _, _, _, dk, dv = state_gla.shape
    hk, hv = N_HEADS * dk, N_HEADS * dv
    rank = w_alpha2.shape[1]
    alpha = (2.0 * depth) ** 0.25
    blk = min(MAIN_BLOCK, seq)
    sblk = SMALL_BLOCK
    nseq = MAIN_SEQS if nb % MAIN_SEQS == 0 else 1
    assert n_meta <= sblk and seq_s <= sblk

    hs = jnp.zeros((1 + nbs, sblk, d), F32)
    hs = hs.at[0, :n_meta].set(meta.astype(F32)).at[1:, :seq_s].set(x_sample)
    n_valid_s = jnp.asarray([n_meta] + [seq_s] * nbs, jnp.int32)
    valid_s = (jnp.arange(sblk)[None, :, None] < n_valid_s[:, None, None]).astype(F32)
    valid_s = jnp.broadcast_to(valid_s, (1 + nbs, sblk, dk))
    off = jnp.asarray([0] + [n_meta + PAST_LEN] * nbs, jnp.int32)
    tabs_s = _mix_tables(off[:, None] + jnp.arange(sblk)[None, :], sblk, n_valid_s, dk, dv)
    tabs_p = _mix_tables(n_meta + jnp.arange(seq)[None, :], blk, jnp.asarray([blk], jnp.int32), dk, dv)
    hp = x_prompt

    c_rg = 2 * hk + hv
    c_ag = c_rg + hv
    c_qr = c_ag + rank
    c_gr = c_qr + 2 * hk + hv
    c_mg = c_gr + hv
    w = {
        "mix": jnp.concatenate([w_in[:, :, :c_rg], w_in[:, :, c_qr:c_gr]], axis=2).astype(BF16),
        "ag": jnp.pad(w_in[:, :, c_ag:c_qr], ((0, 0), (0, 0), (0, GATE_RANK_PAD - rank))).astype(BF16),
        "a2": jnp.pad(w_alpha2, ((0, 0), (0, GATE_RANK_PAD - rank), (0, 0))).astype(BF16),
        "g": jnp.concatenate([w_in[:, :, c_rg:c_ag], w_in[:, :, c_gr:c_mg], w_in[:, :, c_mg:]],
                             axis=2).astype(BF16),
        "oa": w_o_gla.astype(BF16),
        "ob": w_o_ret.astype(BF16),
        "o": w_out.astype(BF16),
    }
    w1u, w1d = w_ffn1_up.astype(BF16), w_ffn1_down.astype(BF16)
    w2u, w2d = w_ffn2_up.astype(BF16), w_ffn2_down.astype(BF16)

    gla_p, ret_p, gla_s, ret_s = [], [], [], []
    for l in range(depth):
        vec = {
            "ba": b_alpha[l][None, :],
            "gna": gn_gla[l][None, :],
            "gnr": gn_ret[l][None, :],
            "bm": b_merge[l][None, :],
            "lng": ln_g[l, 1][None, :],
            "lnb": ln_b[l, 1][None, :],
        }

        def layer(h3, gla0, ret0, tabs, bsz, ns, valid):
            shape = h3.shape
            h = _ffn_ln(h3.reshape(-1, d), w1u, w1d, l, ln_g[l, 0][None, :], ln_b[l, 0][None, :], alpha)
            h, sg, sr = _attn(h.reshape(shape), gla0, ret0, w, l, vec, tabs, bsz, ns, alpha, valid)
            h = _ffn_ln(h.reshape(-1, d), w2u, w2d, l, ln_g[l, 2][None, :], ln_b[l, 2][None, :], alpha)
            return h.reshape(shape), sg, sr

        zero = jnp.zeros((1, N_HEADS, dk, dv), F32)
        hs, sg_s, sr_s = layer(hs, jnp.concatenate([zero, state_gla[l].astype(F32)], axis=0),
                               jnp.concatenate([zero, state_ret[l].astype(F32)], axis=0),
                               tabs_s, sblk, 1, valid_s)
        hp, sg_p, sr_p = layer(hp, sg_s[:1], sr_s[:1], tabs_p, blk, nseq, None)
        gla_p.append(sg_p.astype(x_prompt.dtype))
        ret_p.append(sr_p.astype(x_prompt.dtype))
        gla_s.append(sg_s[1:].astype(state_gla.dtype))
        ret_s.append(sr_s[1:].astype(state_ret.dtype))

    return (hp, hs[1:, :seq_s], jnp.stack(gla_p, axis=0), jnp.stack(ret_p, axis=0),
            jnp.stack(gla_s, axis=0), jnp.stack(ret_s, axis=0))
```

```python
import functools

import numpy as np
import jax
import jax.numpy as jnp
from jax import lax
from jax.experimental import pallas as pl
from jax.experimental.pallas import tpu as pltpu

F32 = jnp.float32
BF16 = jnp.bfloat16

N_HEADS = 4
GATE_RANK_PAD = 128
GATE_TAU = 16.0
PAST_LEN = 1024
ROPE_BASE = 10000.0
LN_EPS = 1e-5
GN_EPS = 1e-5
LOG2E = 1.4426950408889634

SUBLANES = 8
MXU_WIDTH = 256
MAIN_BLOCK = 256
MAIN_SEQS = 1
MAIN_SUBBLOCKS = 2
STAGGER = 22
DIAG_BLOCK = 128
GATE_SLOTS = (1, 1, (1, 0, 1, 0, 1, 0, 1, 0), 0, (1, 1, 0, 0), 0)
SMALL_BLOCK = 32
ROW_TILE = 1024
ROW_SUB = 256
FFN_CHUNK = 6 * MXU_WIDTH
VMEM_LIMIT = 56 * 1024 * 1024

_NT = (((1,), (1,)), ((), ()))
_TN = (((0,), (0,)), ((), ()))


def _dot(a, b):
    return jnp.dot(a, b, preferred_element_type=F32)


def _dot_nt(a, b):
    return lax.dot_general(a, b, _NT, preferred_element_type=F32)


def _dot_tn(a, b):
    return lax.dot_general(a, b, _TN, preferred_element_type=F32)


def _layer_norm(x, g, b):
    mu = jnp.mean(x, axis=-1, keepdims=True)
    xc = x - mu
    var = jnp.mean(xc * xc, axis=-1, keepdims=True)
    return xc * lax.rsqrt(var + LN_EPS) * g + b


def _sigmoid(x):
    return 0.5 + 0.5 * jnp.tanh(0.5 * x)


def _silu(x):
    hx = 0.5 * x
    return hx + hx * jnp.tanh(hx)


def _log_sigmoid(x):
    return jnp.minimum(x, 0.0) - jnp.log1p(jnp.exp(-jnp.abs(x)))


def _resident(shape):
    return pl.BlockSpec(shape, lambda *_: (0,) * len(shape), pipeline_mode=pl.Buffered(1))


def _layer_of(stacked):
    rest = stacked.shape[1:]
    return pl.BlockSpec((None,) + rest, lambda *a: (a[-1][0],) + (0,) * len(rest),
                        pipeline_mode=pl.Buffered(1))


def _row_tile(n_rows, limit):
    t = min(limit, n_rows)
    while n_rows % t or t % SUBLANES:
        t -= SUBLANES
    return t


def _ffn_ln_kernel(_layer_ref, h_ref, wup_ref, wdn_ref, g_ref, b_ref, o_ref, *, d_ff, cuts, sub, alpha):
    n_sub = h_ref.shape[0] // sub
    n_ch = len(cuts) - 1
    items = [(s, j) for s in range(n_sub) for j in range(n_ch)]
    hb = [None] * n_sub
    acc = [None] * n_sub

    def up(s, j):
        if hb[s] is None:
            hb[s] = h_ref[s * sub:(s + 1) * sub, :].astype(BF16)
        return (_dot(hb[s], wup_ref[:, cuts[j]:cuts[j + 1]]),
                _dot(hb[s], wup_ref[:, d_ff + cuts[j]:d_ff + cuts[j + 1]]))

    def finish(s):
        rows = slice(s * sub, (s + 1) * sub)
        o_ref[rows, :] = _layer_norm(alpha * h_ref[rows, :] + 0.5 * acc[s], g_ref[...], b_ref[...])

    nxt = up(*items[0])
    ready = None
    for idx, (s, j) in enumerate(items):
        a, b = nxt
        if idx + 1 < len(items):
            nxt = up(*items[idx + 1])
        if ready is not None:
            finish(ready)
            ready = None
        y = _dot((_silu(a) * b).astype(BF16), wdn_ref[cuts[j]:cuts[j + 1], :])
        acc[s] = y if acc[s] is None else acc[s] + y
        if j == n_ch - 1:
            ready = s
    finish(ready)


def _ffn_ln(h, w_up, w_down, layer, g, b, alpha):
    n, d = h.shape
    d_ff = w_down.shape[1]
    cuts = tuple(range(0, d_ff, FFN_CHUNK)) + (d_ff,)
    tm = _row_tile(n, ROW_TILE)
    sub = ROW_SUB if tm % ROW_SUB == 0 else tm
    return pl.pallas_call(
        functools.partial(_ffn_ln_kernel, d_ff=d_ff, cuts=cuts, sub=sub, alpha=alpha),
        grid_spec=pltpu.PrefetchScalarGridSpec(
            num_scalar_prefetch=1,
            grid=(n // tm,),
            in_specs=[
                pl.BlockSpec((tm, d), lambda i, l: (i, 0)),
                _layer_of(w_up),
                _layer_of(w_down),
                _resident(g.shape),
                _resident(b.shape),
            ],
            out_specs=pl.BlockSpec((tm, d), lambda i, l: (i, 0)),
        ),
        out_shape=jax.ShapeDtypeStruct((n, d), F32),
        compiler_params=pltpu.CompilerParams(
            dimension_semantics=("parallel",), vmem_limit_bytes=VMEM_LIMIT),
        name="ffn_ln",
    )(jnp.asarray([layer], jnp.int32), h, w_up, w_down, g, b)


def _head_norm(o, gain):
    mu = jnp.mean(o, axis=-1, keepdims=True)
    oc = o - mu
    var = jnp.mean(oc * oc, axis=-1, keepdims=True)
    return oc * lax.rsqrt(var + GN_EPS) * gain


def _attn_kernel(_layer_ref, *refs, nseq, nsub, stagger, blk, dblk, dk, dv, masked, alpha):
    n_in = 24 if masked else 23
    (h_ref, wmix_ref, wag_ref, wa2_ref, ba_ref, cos_ref, sin_ref, decq_ref, deck_ref, tri_ref,
     lvl_ref, gla0_ref, ret0_ref, gna_ref, gnr_ref, gpow_ref, wg_ref, bm_ref, woa_ref, wob_ref,
     wo_ref, lng_ref, lnb_ref) = refs[:23]
    valid_ref = refs[23] if masked else None
    o_ref, sg_ref, sr_ref, sgt_scr, sr_scr, b_scr, oa_scr, ob_scr = refs[n_in:]
    hk = N_HEADS * dk
    hv = N_HEADS * dv
    d = h_ref.shape[-1]
    n_diag = blk // dblk
    lv_diag = dblk.bit_length() - 1
    n_lv = blk.bit_length() - 1
    c0 = 2 * hk + hv
    pw = d // 2
    n_gate = 4 * d // pw
    j = pl.program_id(1)

    def shared(ref, s):
        return min(s, ref.shape[0] - 1)

    @pl.when(j == 0)
    def _():
        for s in range(nseq):
            for hd in range(N_HEADS):
                sgt_scr[s, hd] = gla0_ref[shared(gla0_ref, s), hd].T
            sr_scr[s] = ret0_ref[shared(ret0_ref, s)]

    lvl = lvl_ref[...]
    r8 = lax.broadcasted_iota(jnp.int32, (blk, hk), 0) & (SUBLANES - 1)
    causal = (lax.broadcasted_iota(jnp.int32, (blk, blk), 0)
              >= lax.broadcasted_iota(jnp.int32, (blk, blk), 1))

    def tile(x, i, hd):
        return x[i * dblk:(i + 1) * dblk, hd * dk:(hd + 1) * dk]

    def block(s, u):
        rows = slice(u * blk, (u + 1) * blk)
        su = s * nsub + u
        h = h_ref[s, rows, :]
        hb = h.astype(BF16)
        gates = []

        def gate_pieces(n):
            for _ in range(n):
                p = len(gates)
                if p == n_gate:
                    return
                x = _dot(hb, wg_ref[:, p * pw:(p + 1) * pw])
                if p * pw < 2 * d:
                    gates.append(_silu(x))
                else:
                    gates.append(_sigmoid(x + bm_ref[:, p * pw - 2 * d:(p + 1) * pw - 2 * d]))

        def slot(k, i):
            n = GATE_SLOTS[k]
            return n if isinstance(n, int) else (n[i] if i < len(n) else 0)

        ag = _dot(hb, wag_ref[...])
        qg = _dot(hb, wmix_ref[:, 0:hk]) * (dk ** -0.5)
        yield
        gate_in = _dot(ag.astype(BF16), wa2_ref[...])
        kg = _dot(hb, wmix_ref[:, hk:2 * hk])
        vg = _dot(hb, wmix_ref[:, 2 * hk:2 * hk + hv]).astype(BF16)
        gate_pieces(slot(0, 0))
        lg = _log_sigmoid(gate_in + ba_ref[...]) / GATE_TAU
        if masked:
            valid = valid_ref[s, rows, :]
            vmask = jnp.concatenate([valid] * N_HEADS, axis=1)
            lg = lg * vmask
            kg = kg * vmask
        yield
        qr = _dot(hb, wmix_ref[:, c0:c0 + hk])
        kr = _dot(hb, wmix_ref[:, c0 + hk:c0 + 2 * hk])
        lg_hi = lg.astype(BF16)
        rem = lg - lg_hi.astype(F32)
        lg_mid = rem.astype(BF16)
        lg_lo = (rem - lg_mid.astype(F32)).astype(BF16)
        tri = tri_ref[...]
        b2 = (_dot(tri, lg_hi) + _dot(tri, lg_mid) + _dot(tri, lg_lo)) * LOG2E
        b_scr[su] = b2
        vr = _dot(hb, wmix_ref[:, c0 + 2 * hk:c0 + 2 * hk + hv]).astype(BF16)
        gate_pieces(slot(1, 0))
        yield

        def in_group_rows(r):
            return jnp.concatenate(
                [jnp.broadcast_to(b_scr[su, pl.ds(SUBLANES * i + r, 1), :], (SUBLANES, hk))
                 for i in range(blk // SUBLANES)], axis=0)

        def level_factor(lv):
            if lv == 0:
                a = jnp.where(r8 < 2, in_group_rows(0),
                              jnp.where(r8 < 4, in_group_rows(2),
                                        jnp.where(r8 < 6, in_group_rows(4), in_group_rows(6))))
                diff = b2 - a
            elif lv == 1:
                diff = b2 - jnp.where(r8 < 4, in_group_rows(1), in_group_rows(5))
            elif lv == 2:
                diff = b2 - in_group_rows(3)
            else:
                grp = 2 << lv
                diff = jnp.concatenate(
                    [b2[g * grp:(g + 1) * grp] - b_scr[su, pl.ds(g * grp + grp // 2 - 1, 1), :]
                     for g in range(blk // grp)], axis=0)
            return jnp.exp2(-jnp.abs(diff))

        fac = []
        for lv in range(n_lv):
            gate_pieces(slot(2, lv))
            fac.append(level_factor(lv))
            yield
        last = b_scr[su, pl.ds(blk - 1, 1), :]
        q_in = (qg * jnp.exp2(b2)).astype(BF16)
        k_out = (kg * jnp.exp2(-jnp.abs(last - b2))).astype(BF16)
        s_dec = jnp.exp2(last)
        yield

        qgb = qg.astype(BF16)
        kgb = kg.astype(BF16)
        on_diag = lvl == lv_diag
        att = [[jnp.where(on_diag, _dot_nt(tile(qgb, i, hd), tile(kgb, i, hd)), 0.0)
                for hd in range(N_HEADS)] for i in range(n_diag)]
        yield
        for lv in range(lv_diag):
            qf = (qg * fac[lv]).astype(BF16)
            kf = (kg * fac[lv]).astype(BF16)
            at_level = lvl == lv
            for i in range(n_diag):
                for hd in range(N_HEADS):
                    att[i][hd] = jnp.where(at_level, _dot_nt(tile(qf, i, hd), tile(kf, i, hd)), att[i][hd])
            gate_pieces(slot(3, lv))
            yield
        if n_diag == 2:
            qf = (qg[dblk:] * fac[lv_diag][dblk:]).astype(BF16)
            kf = (kg[:dblk] * fac[lv_diag][:dblk]).astype(BF16)
        gna = gna_ref[...]
        for hd in range(N_HEADS):
            ks = slice(hd * dk, (hd + 1) * dk)
            vs = slice(hd * dv, (hd + 1) * dv)
            v = vg[:, vs]
            if n_diag == 2:
                below = _dot_nt(qf[:, ks], kf[:, ks])
                o = jnp.concatenate([
                    _dot(att[0][hd].astype(BF16), v[:dblk]),
                    _dot(jnp.concatenate([below, att[1][hd]], axis=1).astype(BF16), v)], axis=0)
            else:
                o = _dot(att[0][hd].astype(BF16), v)
            st = sgt_scr[s, hd]
            o = o + _dot_nt(q_in[:, ks], st.astype(BF16))
            sgt_scr[s, hd] = st * s_dec[:, ks] + _dot_tn(v, k_out[:, ks])
            oa_scr[su, :, vs] = _head_norm(o, gna[:, vs])
            gate_pieces(slot(4, hd))
            yield

        cos = cos_ref[shared(cos_ref, s), rows, :]
        sin = sin_ref[shared(sin_ref, s), rows, :]
        decq = decq_ref[...]
        deck = deck_ref[...]
        gpow = gpow_ref[shared(gpow_ref, s)]
        gnr = gnr_ref[...]
        for hd in range(N_HEADS):
            ks = slice(hd * dk, (hd + 1) * dk)
            vs = slice(hd * dv, (hd + 1) * dv)
            q = qr[:, ks]
            k = kr[:, ks]
            q = (q * cos + pltpu.roll(q, dk // 2, 1) * sin) * decq[:, ks]
            k = (k * cos + pltpu.roll(k, dk // 2, 1) * sin) * deck[:, ks]
            if masked:
                k = k * valid
            qb = q.astype(BF16)
            kb = k.astype(BF16)
            v = vr[:, vs]
            sc = jnp.where(causal, _dot_nt(qb, kb), 0.0)
            s0 = sr_scr[s, hd]
            o = _dot(sc.astype(BF16), v) + _dot(qb, s0.astype(BF16))
            sr_scr[s, hd] = (s0 + _dot_tn(kb, v)) * gpow[:, vs]
            ob_scr[su, :, vs] = _head_norm(o, gnr[:, vs])
            gate_pieces(slot(5, hd))
            yield

        gate_pieces(n_gate)
        rg, gr, ga, gb = [jnp.concatenate(gates[i * (d // pw):(i + 1) * (d // pw)], axis=1)
                          for i in range(4)]
        ya = _dot((rg * oa_scr[su]).astype(BF16), woa_ref[...])
        yb = _dot((gr * ob_scr[su]).astype(BF16), wob_ref[...])
        y = _dot((ga * ya + gb * yb).astype(BF16), wo_ref[...])
        o_ref[s, rows, :] = _layer_norm(alpha * h + y, lng_ref[...], lnb_ref[...])
        yield

    waiting = [block(s, u) for u in range(nsub) for s in range(nseq)]
    running = []
    tick = 0
    while waiting or running:
        if waiting and tick % stagger == 0:
            running.append(waiting.pop(0))
        for g in list(running):
            try:
                next(g)
            except StopIteration:
                running.remove(g)
        tick += 1

    @pl.when(j == pl.num_programs(1) - 1)
    def _():
        for s in range(nseq):
            for hd in range(N_HEADS):
                sg_ref[s, hd] = sgt_scr[s, hd].T
            sr_ref[s] = sr_scr[s]


def _level_matrix(n):
    r = np.arange(n)[:, None]
    c = np.arange(n)[None, :]
    x = np.maximum(r ^ c, 1)
    lvl = np.where(r > c, np.floor(np.log2(x)), np.where(r == c, n.bit_length() - 1, -1))
    return lvl.astype(np.int32)


def _attn(h, gla0, ret0, w, layer, vec, tabs, blk, nseq, alpha, valid=None):
    nb, t, d = h.shape
    _, _, dk, dv = gla0.shape
    hk, hv = N_HEADS * dk, N_HEADS * dv
    dblk = min(blk, DIAG_BLOCK)
    nsub = MAIN_SUBBLOCKS if (valid is None and t % (MAIN_SUBBLOCKS * blk) == 0) else 1
    tstep = nsub * blk
    assert t % tstep == 0 and blk & (blk - 1) == 0 and SUBLANES <= blk <= 256 and blk // dblk <= 2
    assert nb % nseq == 0
    lvl = jnp.asarray(_level_matrix(dblk))
    tri = jnp.asarray(np.tril(np.ones((blk, blk), np.float32)), BF16)
    masked = valid is not None

    def per_seq(a, width):
        if a.shape[0] > 1:
            return pl.BlockSpec((nseq, tstep, width), lambda b, j, l: (b, j, 0))
        return pl.BlockSpec((1, tstep, width), lambda b, j, l: (0, j, 0))

    def state_in(a):
        if a.shape[0] > 1:
            return pl.BlockSpec((nseq, N_HEADS, dk, dv), lambda b, j, l: (b, 0, 0, 0))
        return pl.BlockSpec((1, N_HEADS, dk, dv), lambda b, j, l: (0, 0, 0, 0))

    seq = pl.BlockSpec((nseq, tstep, d), lambda b, j, l: (b, j, 0))
    args = [h, w["mix"], w["ag"], w["a2"], vec["ba"], tabs["cos"], tabs["sin"], tabs["decq"], tabs["deck"],
            tri, lvl, gla0, ret0, vec["gna"], vec["gnr"], tabs["gpow"], w["g"], vec["bm"], w["oa"], w["ob"],
            w["o"], vec["lng"], vec["lnb"]]
    in_specs = [
        seq,
        _layer_of(w["mix"]), _layer_of(w["ag"]), _layer_of(w["a2"]),
        _resident(vec["ba"].shape),
        per_seq(tabs["cos"], dk), per_seq(tabs["sin"], dk),
        _resident(tabs["decq"].shape), _resident(tabs["deck"].shape),
        _resident(tri.shape), _resident(lvl.shape),
        state_in(gla0), state_in(ret0),
        _resident(vec["gna"].shape), _resident(vec["gnr"].shape),
        (pl.BlockSpec((nseq, 1, hv), lambda b, j, l: (b, 0, 0)) if tabs["gpow"].shape[0] > 1
         else pl.BlockSpec((1, 1, hv), lambda b, j, l: (0, 0, 0))),
        _layer_of(w["g"]), _resident(vec["bm"].shape), _layer_of(w["oa"]),
        _layer_of(w["ob"]), _layer_of(w["o"]), _resident(vec["lng"].shape),
        _resident(vec["lnb"].shape),
    ]
    if masked:
        in_specs.append(pl.BlockSpec((nseq, tstep, dk), lambda b, j, l: (b, j, 0)))
        args.append(valid)
    state_out = pl.BlockSpec((nseq, N_HEADS, dk, dv), lambda b, j, l: (b, 0, 0, 0))
    return pl.pallas_call(
        functools.partial(_attn_kernel, nseq=nseq, nsub=nsub, stagger=STAGGER, blk=blk, dblk=dblk, dk=dk, dv=dv,
                          masked=masked, alpha=alpha),
        grid_spec=pltpu.PrefetchScalarGridSpec(
            num_scalar_prefetch=1,
            grid=(nb // nseq, t // tstep),
            in_specs=in_specs,
            out_specs=[seq, state_out, state_out],
            scratch_shapes=[
                pltpu.VMEM((nseq, N_HEADS, dv, dk), F32), pltpu.VMEM((nseq, N_HEADS, dk, dv), F32),
                pltpu.VMEM((nseq * nsub, blk, hk), F32), pltpu.VMEM((nseq * nsub, blk, hv), F32),
                pltpu.VMEM((nseq * nsub, blk, hv), F32)],
        ),
        out_shape=[jax.ShapeDtypeStruct((nb, t, d), F32),
                   jax.ShapeDtypeStruct((nb, N_HEADS, dk, dv), F32),
                   jax.ShapeDtypeStruct((nb, N_HEADS, dk, dv), F32)],
        compiler_params=pltpu.CompilerParams(
            dimension_semantics=("parallel", "arbitrary"), vmem_limit_bytes=VMEM_LIMIT),
        name="attn",
    )(jnp.asarray([layer], jnp.int32), *args)


def _mix_tables(pos, blk, n_valid, dk, dv):
    half = dk // 2
    inv = ROPE_BASE ** (-jnp.arange(half, dtype=F32) / half)
    ang = pos.astype(F32)[..., None] * inv
    cos = jnp.cos(ang)
    sin = jnp.sin(ang)
    log_gamma = jnp.log1p(-(2.0 ** (-5.0 - jnp.arange(N_HEADS, dtype=F32))))
    step = jnp.arange(blk, dtype=F32)[:, None, None] + 1.0
    dec = jnp.broadcast_to(step * log_gamma[None, :, None], (blk, N_HEADS, dk)).reshape(blk, N_HEADS * dk)
    gpow = jnp.exp(n_valid.astype(F32)[:, None, None] * log_gamma[None, :, None])
    gpow = jnp.broadcast_to(gpow, (pos.shape[0], N_HEADS, dv)).reshape(pos.shape[0], 1, N_HEADS * dv)
    return {
        "cos": jnp.concatenate([cos, cos], axis=-1),
        "sin": jnp.concatenate([-sin, sin], axis=-1),
        "decq": jnp.exp(dec),
        "deck": jnp.exp(-dec) * dk ** -0.5,
        "gpow": gpow,
    }


def kernel(x_prompt, x_sample, state_gla, state_ret, meta, ln_g, ln_b, w_ffn1_up, w_ffn1_down, w_in,
           w_alpha2, b_alpha, b_merge, gn_gla, gn_ret, w_o_gla, w_o_ret, w_out, w_ffn2_up, w_ffn2_down):
    nb, seq, d = x_prompt.shape
    nbs, seq_s, _ = x_sample.shape
    depth = ln_g.shape[0]
    n_meta = meta.shape[0]
    _, _, _, dk, dv = state_gla.shape
    hk, hv = N_HEADS * dk, N_HEADS * dv
    rank = w_alpha2.shape[1]
    alpha = (2.0 * depth) ** 0.25
    blk = min(MAIN_BLOCK, seq)
    sblk = SMALL_BLOCK
    nseq = MAIN_SEQS if nb % MAIN_SEQS == 0 else 1
    assert n_meta <= sblk and seq_s <= sblk

    hs = jnp.zeros((1 + nbs, sblk, d), F32)
    hs = hs.at[0, :n_meta].set(meta.astype(F32)).at[1:, :seq_s].set(x_sample)
    n_valid_s = jnp.asarray([n_meta] + [seq_s] * nbs, jnp.int32)
    valid_s = (jnp.arange(sblk)[None, :, None] < n_valid_s[:, None, None]).astype(F32)
    valid_s = jnp.broadcast_to(valid_s, (1 + nbs, sblk, dk))
    off = jnp.asarray([0] + [n_meta + PAST_LEN] * nbs, jnp.int32)
    tabs_s = _mix_tables(off[:, None] + jnp.arange(sblk)[None, :], sblk, n_valid_s, dk, dv)
    tabs_p = _mix_tables(n_meta + jnp.arange(seq)[None, :], blk, jnp.asarray([blk], jnp.int32), dk, dv)
    hp = x_prompt

    c_rg = 2 * hk + hv
    c_ag = c_rg + hv
    c_qr = c_ag + rank
    c_gr = c_qr + 2 * hk + hv
    c_mg = c_gr + hv
    w = {
        "mix": jnp.concatenate([w_in[:, :, :c_rg], w_in[:, :, c_qr:c_gr]], axis=2).astype(BF16),
        "ag": jnp.pad(w_in[:, :, c_ag:c_qr], ((0, 0), (0, 0), (0, GATE_RANK_PAD - rank))).astype(BF16),
        "a2": jnp.pad(w_alpha2, ((0, 0), (0, GATE_RANK_PAD - rank), (0, 0))).astype(BF16),
        "g": jnp.concatenate([w_in[:, :, c_rg:c_ag], w_in[:, :, c_gr:c_mg], w_in[:, :, c_mg:]],
                             axis=2).astype(BF16),
        "oa": w_o_gla.astype(BF16),
        "ob": w_o_ret.astype(BF16),
        "o": w_out.astype(BF16),
    }
    w1u, w1d = w_ffn1_up.astype(BF16), w_ffn1_down.astype(BF16)
    w2u, w2d = w_ffn2_up.astype(BF16), w_ffn2_down.astype(BF16)

    gla_p, ret_p, gla_s, ret_s = [], [], [], []
    for l in range(depth):
        vec = {
            "ba": b_alpha[l][None, :],
            "gna": gn_gla[l][None, :],
            "gnr": gn_ret[l][None, :],
            "bm": b_merge[l][None, :],
            "lng": ln_g[l, 1][None, :],
            "lnb": ln_b[l, 1][None, :],
        }

        def layer(h3, gla0, ret0, tabs, bsz, ns, valid):
            shape = h3.shape
            h = _ffn_ln(h3.reshape(-1, d), w1u, w1d, l, ln_g[l, 0][None, :], ln_b[l, 0][None, :], alpha)
            h, sg, sr = _attn(h.reshape(shape), gla0, ret0, w, l, vec, tabs, bsz, ns, alpha, valid)
            h = _ffn_ln(h.reshape(-1, d), w2u, w2d, l, ln_g[l, 2][None, :], ln_b[l, 2][None, :], alpha)
            return h.reshape(shape), sg, sr

        zero = jnp.zeros((1, N_HEADS, dk, dv), F32)
        hs, sg_s, sr_s = layer(hs, jnp.concatenate([zero, state_gla[l].astype(F32)], axis=0),
                               jnp.concatenate([zero, state_ret[l].astype(F32)], axis=0),
                               tabs_s, sblk, 1, valid_s)
        hp, sg_p, sr_p = layer(hp, sg_s[:1], sr_s[:1], tabs_p, blk, nseq, None)
        gla_p.append(sg_p.astype(x_prompt.dtype))
        ret_p.append(sr_p.astype(x_prompt.dtype))
        gla_s.append(sg_s[1:].astype(state_gla.dtype))
        ret_s.append(sr_s[1:].astype(state_ret.dtype))

    return (hp, hs[1:, :seq_s], jnp.stack(gla_p, axis=0), jnp.stack(ret_p, axis=0),
            jnp.stack(gla_s, axis=0), jnp.stack(ret_s, axis=0))
```

```python
import functools

import numpy as np
import jax
import jax.numpy as jnp
from jax import lax
from jax.experimental import pallas as pl
from jax.experimental.pallas import tpu as pltpu

F32 = jnp.float32
BF16 = jnp.bfloat16

N_HEADS = 4
GATE_RANK_PAD = 128
GATE_TAU = 16.0
PAST_LEN = 1024
ROPE_BASE = 10000.0
LN_EPS = 1e-5
GN_EPS = 1e-5
LOG2E = 1.4426950408889634

SUBLANES = 8
MXU_WIDTH = 256
MAIN_BLOCK = 256
MAIN_SEQS = 1
MAIN_SUBBLOCKS = 2
STAGGER = 22
DIAG_BLOCK = 128
GATE_SLOTS = (1, 1, (1, 0, 1, 0, 1, 0, 1, 0), 0, (1, 1, 0, 0), 0)
SMALL_BLOCK = 32
ROW_TILE = 1024
ROW_SUB = 256
FFN_CHUNK = 6 * MXU_WIDTH
VMEM_LIMIT = 56 * 1024 * 1024

_NT = (((1,), (1,)), ((), ()))
_TN = (((0,), (0,)), ((), ()))


def _dot(a, b):
    return jnp.dot(a, b, preferred_element_type=F32)


def _dot_nt(a, b):
    return lax.dot_general(a, b, _NT, preferred_element_type=F32)


def _dot_tn(a, b):
    return lax.dot_general(a, b, _TN, preferred_element_type=F32)


def _layer_norm(x, g, b):
    mu = jnp.mean(x, axis=-1, keepdims=True)
    xc = x - mu
    var = jnp.mean(xc * xc, axis=-1, keepdims=True)
    return xc * lax.rsqrt(var + LN_EPS) * g + b


def _sigmoid(x):
    return 0.5 + 0.5 * jnp.tanh(0.5 * x)


def _silu(x):
    hx = 0.5 * x
    return hx + hx * jnp.tanh(hx)


def _log_sigmoid(x):
    return jnp.minimum(x, 0.0) - jnp.log1p(jnp.exp(-jnp.abs(x)))


def _resident(shape):
    return pl.BlockSpec(shape, lambda *_: (0,) * len(shape), pipeline_mode=pl.Buffered(1))


def _layer_of(stacked):
    rest = stacked.shape[1:]
    return pl.BlockSpec((None,) + rest, lambda *a: (a[-1][0],) + (0,) * len(rest),
                        pipeline_mode=pl.Buffered(1))


def _row_tile(n_rows, limit):
    t = min(limit, n_rows)
    while n_rows % t or t % SUBLANES:
        t -= SUBLANES
    return t


def _ffn_ln_kernel(_layer_ref, h_ref, wup_ref, wdn_ref, g_ref, b_ref, o_ref, *, d_ff, cuts, sub, alpha):
    n_sub = h_ref.shape[0] // sub
    n_ch = len(cuts) - 1
    items = [(s, j) for s in range(n_sub) for j in range(n_ch)]
    hb = [None] * n_sub
    acc = [None] * n_sub

    def up(s, j):
        if hb[s] is None:
            hb[s] = h_ref[s * sub:(s + 1) * sub, :].astype(BF16)
        return (_dot(hb[s], wup_ref[:, cuts[j]:cuts[j + 1]]),
                _dot(hb[s], wup_ref[:, d_ff + cuts[j]:d_ff + cuts[j + 1]]))

    def finish(s):
        rows = slice(s * sub, (s + 1) * sub)
        o_ref[rows, :] = _layer_norm(alpha * h_ref[rows, :] + 0.5 * acc[s], g_ref[...], b_ref[...])

    nxt = up(*items[0])
    ready = None
    for idx, (s, j) in enumerate(items):
        a, b = nxt
        if idx + 1 < len(items):
            nxt = up(*items[idx + 1])
        if ready is not None:
            finish(ready)
            ready = None
        y = _dot((_silu(a) * b).astype(BF16), wdn_ref[cuts[j]:cuts[j + 1], :])
        acc[s] = y if acc[s] is None else acc[s] + y
        if j == n_ch - 1:
            ready = s
    finish(ready)


def _ffn_ln(h, w_up, w_down, layer, g, b, alpha):
    n, d = h.shape
    d_ff = w_down.shape[1]
    cuts = tuple(range(0, d_ff, FFN_CHUNK)) + (d_ff,)
    tm = _row_tile(n, ROW_TILE)
    sub = ROW_SUB if tm % ROW_SUB == 0 else tm
    return pl.pallas_call(
        functools.partial(_ffn_ln_kernel, d_ff=d_ff, cuts=cuts, sub=sub, alpha=alpha),
        grid_spec=pltpu.PrefetchScalarGridSpec(
            num_scalar_prefetch=1,
            grid=(n // tm,),
            in_specs=[
                pl.BlockSpec((tm, d), lambda i, l: (i, 0)),
                _layer_of(w_up),
                _layer_of(w_down),
                _resident(g.shape),
                _resident(b.shape),
            ],
            out_specs=pl.BlockSpec((tm, d), lambda i, l: (i, 0)),
        ),
        out_shape=jax.ShapeDtypeStruct((n, d), F32),
        compiler_params=pltpu.CompilerParams(
            dimension_semantics=("parallel",), vmem_limit_bytes=VMEM_LIMIT),
        name="ffn_ln",
    )(jnp.asarray([layer], jnp.int32), h, w_up, w_down, g, b)


def _head_norm(o, gain):
    mu = jnp.mean(o, axis=-1, keepdims=True)
    oc = o - mu
    var = jnp.mean(oc * oc, axis=-1, keepdims=True)
    return oc * lax.rsqrt(var + GN_EPS) * gain


def _attn_kernel(_layer_ref, *refs, nseq, nsub, stagger, blk, dblk, dk, dv, masked, alpha):
    n_in = 24 if masked else 23
    (h_ref, wmix_ref, wag_ref, wa2_ref, ba_ref, cos_ref, sin_ref, decq_ref, deck_ref, tri_ref,
     lvl_ref, gla0_ref, ret0_ref, gna_ref, gnr_ref, gpow_ref, wg_ref, bm_ref, woa_ref, wob_ref,
     wo_ref, lng_ref, lnb_ref) = refs[:23]
    valid_ref = refs[23] if masked else None
    o_ref, sg_ref, sr_ref, sgt_scr, sr_scr, b_scr, oa_scr, ob_scr = refs[n_in:]
    hk = N_HEADS * dk
    hv = N_HEADS * dv
    d = h_ref.shape[-1]
    n_diag = blk // dblk
    lv_diag = dblk.bit_length() - 1
    n_lv = blk.bit_length() - 1
    c0 = 2 * hk + hv
    pw = d // 2
    n_gate = 4 * d // pw
    j = pl.program_id(1)

    def shared(ref, s):
        return min(s, ref.shape[0] - 1)

    @pl.when(j == 0)
    def _():
        for s in range(nseq):
            for hd in range(N_HEADS):
                sgt_scr[s, hd] = gla0_ref[shared(gla0_ref, s), hd].T
            sr_scr[s] = ret0_ref[shared(ret0_ref, s)]

    lvl = lvl_ref[...]
    r8 = lax.broadcasted_iota(jnp.int32, (blk, hk), 0) & (SUBLANES - 1)
    causal = (lax.broadcasted_iota(jnp.int32, (blk, blk), 0)
              >= lax.broadcasted_iota(jnp.int32, (blk, blk), 1))

    def tile(x, i, hd):
        return x[i * dblk:(i + 1) * dblk, hd * dk:(hd + 1) * dk]

    def block(s, u):
        rows = slice(u * blk, (u + 1) * blk)
        su = s * nsub + u
        hb = h_ref[s, rows, :].astype(BF16)
        gates = []

        def gate_pieces(n):
            for _ in range(n):
                p = len(gates)
                if p == n_gate:
                    return
                x = _dot(hb, wg_ref[:, p * pw:(p + 1) * pw])
                if p * pw < 2 * d:
                    gates.append(_silu(x))
                else:
                    gates.append(_sigmoid(x + bm_ref[:, p * pw - 2 * d:(p + 1) * pw - 2 * d]))

        def slot(k, i):
            n = GATE_SLOTS[k]
            return n if isinstance(n, int) else (n[i] if i < len(n) else 0)

        ag = _dot(hb, wag_ref[...])
        qg = _dot(hb, wmix_ref[:, 0:hk]) * (dk ** -0.5)
        yield
        gate_in = _dot(ag.astype(BF16), wa2_ref[...])
        kg = _dot(hb, wmix_ref[:, hk:2 * hk])
        vg = _dot(hb, wmix_ref[:, 2 * hk:2 * hk + hv]).astype(BF16)
        gate_pieces(slot(0, 0))
        lg = _log_sigmoid(gate_in + ba_ref[...]) / GATE_TAU
        if masked:
            valid = valid_ref[s, rows, :]
            vmask = jnp.concatenate([valid] * N_HEADS, axis=1)
            lg = lg * vmask
            kg = kg * vmask
        yield
        qr = _dot(hb, wmix_ref[:, c0:c0 + hk])
        kr = _dot(hb, wmix_ref[:, c0 + hk:c0 + 2 * hk])
        lg_hi = lg.astype(BF16)
        rem = lg - lg_hi.astype(F32)
        lg_mid = rem.astype(BF16)
        lg_lo = (rem - lg_mid.astype(F32)).astype(BF16)
        tri = tri_ref[...]
        b2 = (_dot(tri, lg_hi) + _dot(tri, lg_mid) + _dot(tri, lg_lo)) * LOG2E
        b_scr[su] = b2
        vr = _dot(hb, wmix_ref[:, c0 + 2 * hk:c0 + 2 * hk + hv]).astype(BF16)
        gate_pieces(slot(1, 0))
        yield

        def in_group_rows(r):
            return jnp.concatenate(
                [jnp.broadcast_to(b_scr[su, pl.ds(SUBLANES * i + r, 1), :], (SUBLANES, hk))
                 for i in range(blk // SUBLANES)], axis=0)

        def level_factor(lv):
            if lv == 0:
                a = jnp.where(r8 < 2, in_group_rows(0),
                              jnp.where(r8 < 4, in_group_rows(2),
                                        jnp.where(r8 < 6, in_group_rows(4), in_group_rows(6))))
                diff = b2 - a
            elif lv == 1:
                diff = b2 - jnp.where(r8 < 4, in_group_rows(1), in_group_rows(5))
            elif lv == 2:
                diff = b2 - in_group_rows(3)
            else:
                grp = 2 << lv
                diff = jnp.concatenate(
                    [b2[g * grp:(g + 1) * grp] - b_scr[su, pl.ds(g * grp + grp // 2 - 1, 1), :]
                     for g in range(blk // grp)], axis=0)
            return jnp.exp2(-jnp.abs(diff))

        fac = []
        for lv in range(n_lv):
            gate_pieces(slot(2, lv))
            fac.append(level_factor(lv))
            yield
        last = b_scr[su, pl.ds(blk - 1, 1), :]
        q_in = (qg * jnp.exp2(b2)).astype(BF16)
        k_out = (kg * jnp.exp2(-jnp.abs(last - b2))).astype(BF16)
        s_dec = jnp.exp2(last)
        yield

        qgb = qg.astype(BF16)
        kgb = kg.astype(BF16)
        on_diag = lvl == lv_diag
        att = [[jnp.where(on_diag, _dot_nt(tile(qgb, i, hd), tile(kgb, i, hd)), 0.0)
                for hd in range(N_HEADS)] for i in range(n_diag)]
        yield
        for lv in range(lv_diag):
            qf = (qg * fac[lv]).astype(BF16)
            kf = (kg * fac[lv]).astype(BF16)
            at_level = lvl == lv
            for i in range(n_diag):
                for hd in range(N_HEADS):
                    att[i][hd] = jnp.where(at_level, _dot_nt(tile(qf, i, hd), tile(kf, i, hd)), att[i][hd])
            gate_pieces(slot(3, lv))
            yield
        if n_diag == 2:
            qf = (qg[dblk:] * fac[lv_diag][dblk:]).astype(BF16)
            kf = (kg[:dblk] * fac[lv_diag][:dblk]).astype(BF16)
        gna = gna_ref[...]
        for hd in range(N_HEADS):
            ks = slice(hd * dk, (hd + 1) * dk)
            vs = slice(hd * dv, (hd + 1) * dv)
            v = vg[:, vs]
            if n_diag == 2:
                below = _dot_nt(qf[:, ks], kf[:, ks])
                o = jnp.concatenate([
                    _dot(att[0][hd].astype(BF16), v[:dblk]),
                    _dot(jnp.concatenate([below, att[1][hd]], axis=1).astype(BF16), v)], axis=0)
            else:
                o = _dot(att[0][hd].astype(BF16), v)
            st = sgt_scr[s, hd]
            o = o + _dot_nt(q_in[:, ks], st.astype(BF16))
            sgt_scr[s, hd] = st * s_dec[:, ks] + _dot_tn(v, k_out[:, ks])
            oa_scr[su, :, vs] = _head_norm(o, gna[:, vs])
            gate_pieces(slot(4, hd))
            yield

        cos = cos_ref[shared(cos_ref, s), rows, :]
        sin = sin_ref[shared(sin_ref, s), rows, :]
        decq = decq_ref[...]
        deck = deck_ref[...]
        gpow = gpow_ref[shared(gpow_ref, s)]
        gnr = gnr_ref[...]
        for hd in range(N_HEADS):
            ks = slice(hd * dk, (hd + 1) * dk)
            vs = slice(hd * dv, (hd + 1) * dv)
            q = qr[:, ks]
            k = kr[:, ks]
            q = (q * cos + pltpu.roll(q, dk // 2, 1) * sin) * decq[:, ks]
            k = (k * cos + pltpu.roll(k, dk // 2, 1) * sin) * deck[:, ks]
            if masked:
                k = k * valid
            qb = q.astype(BF16)
            kb = k.astype(BF16)
            v = vr[:, vs]
            sc = jnp.where(causal, _dot_nt(qb, kb), 0.0)
            s0 = sr_scr[s, hd]
            o = _dot(sc.astype(BF16), v) + _dot(qb, s0.astype(BF16))
            sr_scr[s, hd] = (s0 + _dot_tn(kb, v)) * gpow[:, vs]
            ob_scr[su, :, vs] = _head_norm(o, gnr[:, vs])
            gate_pieces(slot(5, hd))
            yield

        gate_pieces(n_gate)
        rg, gr, ga, gb = [jnp.concatenate(gates[i * (d // pw):(i + 1) * (d // pw)], axis=1)
                          for i in range(4)]
        ya = _dot((rg * oa_scr[su]).astype(BF16), woa_ref[...])
        yb = _dot((gr * ob_scr[su]).astype(BF16), wob_ref[...])
        y = _dot((ga * ya + gb * yb).astype(BF16), wo_ref[...])
        o_ref[s, rows, :] = _layer_norm(alpha * h_ref[s, rows, :] + y, lng_ref[...], lnb_ref[...])
        yield

    waiting = [block(s, u) for u in range(nsub) for s in range(nseq)]
    running = []
    tick = 0
    while waiting or running:
        if waiting and tick % stagger == 0:
            running.append(waiting.pop(0))
        for g in list(running):
            try:
                next(g)
            except StopIteration:
                running.remove(g)
        tick += 1

    @pl.when(j == pl.num_programs(1) - 1)
    def _():
        for s in range(nseq):
            for hd in range(N_HEADS):
                sg_ref[s, hd] = sgt_scr[s, hd].T
            sr_ref[s] = sr_scr[s]


def _level_matrix(n):
    r = np.arange(n)[:, None]
    c = np.arange(n)[None, :]
    x = np.maximum(r ^ c, 1)
    lvl = np.where(r > c, np.floor(np.log2(x)), np.where(r == c, n.bit_length() - 1, -1))
    return lvl.astype(np.int32)


def _attn(h, gla0, ret0, w, layer, vec, tabs, blk, nseq, alpha, valid=None):
    nb, t, d = h.shape
    _, _, dk, dv = gla0.shape
    hk, hv = N_HEADS * dk, N_HEADS * dv
    dblk = min(blk, DIAG_BLOCK)
    nsub = MAIN_SUBBLOCKS if (valid is None and t % (MAIN_SUBBLOCKS * blk) == 0) else 1
    tstep = nsub * blk
    assert t % tstep == 0 and blk & (blk - 1) == 0 and SUBLANES <= blk <= 256 and blk // dblk <= 2
    assert nb % nseq == 0
    lvl = jnp.asarray(_level_matrix(dblk))
    tri = jnp.asarray(np.tril(np.ones((blk, blk), np.float32)), BF16)
    masked = valid is not None

    def per_seq(a, width):
        if a.shape[0] > 1:
            return pl.BlockSpec((nseq, tstep, width), lambda b, j, l: (b, j, 0))
        return pl.BlockSpec((1, tstep, width), lambda b, j, l: (0, j, 0))

    def state_in(a):
        if a.shape[0] > 1:
            return pl.BlockSpec((nseq, N_HEADS, dk, dv), lambda b, j, l: (b, 0, 0, 0))
        return pl.BlockSpec((1, N_HEADS, dk, dv), lambda b, j, l: (0, 0, 0, 0))

    seq = pl.BlockSpec((nseq, tstep, d), lambda b, j, l: (b, j, 0))
    args = [h, w["mix"], w["ag"], w["a2"], vec["ba"], tabs["cos"], tabs["sin"], tabs["decq"], tabs["deck"],
            tri, lvl, gla0, ret0, vec["gna"], vec["gnr"], tabs["gpow"], w["g"], vec["bm"], w["oa"], w["ob"],
            w["o"], vec["lng"], vec["lnb"]]
    in_specs = [
        seq,
        _layer_of(w["mix"]), _layer_of(w["ag"]), _layer_of(w["a2"]),
        _resident(vec["ba"].shape),
        per_seq(tabs["cos"], dk), per_seq(tabs["sin"], dk),
        _resident(tabs["decq"].shape), _resident(tabs["deck"].shape),
        _resident(tri.shape), _resident(lvl.shape),
        state_in(gla0), state_in(ret0),
        _resident(vec["gna"].shape), _resident(vec["gnr"].shape),
        (pl.BlockSpec((nseq, 1, hv), lambda b, j, l: (b, 0, 0)) if tabs["gpow"].shape[0] > 1
         else pl.BlockSpec((1, 1, hv), lambda b, j, l: (0, 0, 0))),
        _layer_of(w["g"]), _resident(vec["bm"].shape), _layer_of(w["oa"]),
        _layer_of(w["ob"]), _layer_of(w["o"]), _resident(vec["lng"].shape),
        _resident(vec["lnb"].shape),
    ]
    if masked:
        in_specs.append(pl.BlockSpec((nseq, tstep, dk), lambda b, j, l: (b, j, 0)))
        args.append(valid)
    state_out = pl.BlockSpec((nseq, N_HEADS, dk, dv), lambda b, j, l: (b, 0, 0, 0))
    return pl.pallas_call(
        functools.partial(_attn_kernel, nseq=nseq, nsub=nsub, stagger=STAGGER, blk=blk, dblk=dblk, dk=dk, dv=dv,
                          masked=masked, alpha=alpha),
        grid_spec=pltpu.PrefetchScalarGridSpec(
            num_scalar_prefetch=1,
            grid=(nb // nseq, t // tstep),
            in_specs=in_specs,
            out_specs=[seq, state_out, state_out],
            scratch_shapes=[
                pltpu.VMEM((nseq, N_HEADS, dv, dk), F32), pltpu.VMEM((nseq, N_HEADS, dk, dv), F32),
                pltpu.VMEM((nseq * nsub, blk, hk), F32), pltpu.VMEM((nseq * nsub, blk, hv), F32),
                pltpu.VMEM((nseq * nsub, blk, hv), F32)],
        ),
        out_shape=[jax.ShapeDtypeStruct((nb, t, d), F32),
                   jax.ShapeDtypeStruct((nb, N_HEADS, dk, dv), F32),
                   jax.ShapeDtypeStruct((nb, N_HEADS, dk, dv), F32)],
        compiler_params=pltpu.CompilerParams(
            dimension_semantics=("parallel", "arbitrary"), vmem_limit_bytes=VMEM_LIMIT),
        name="attn",
    )(jnp.asarray([layer], jnp.int32), *args)


def _mix_tables(pos, blk, n_valid, dk, dv):
    half = dk // 2
    inv = ROPE_BASE ** (-jnp.arange(half, dtype=F32) / half)
    ang = pos.astype(F32)[..., None] * inv
    cos = jnp.cos(ang)
    sin = jnp.sin(ang)
    log_gamma = jnp.log1p(-(2.0 ** (-5.0 - jnp.arange(N_HEADS, dtype=F32))))
    step = jnp.arange(blk, dtype=F32)[:, None, None] + 1.0
    dec = jnp.broadcast_to(step * log_gamma[None, :, None], (blk, N_HEADS, dk)).reshape(blk, N_HEADS * dk)
    gpow = jnp.exp(n_valid.astype(F32)[:, None, None] * log_gamma[None, :, None])
    gpow = jnp.broadcast_to(gpow, (pos.shape[0], N_HEADS, dv)).reshape(pos.shape[0], 1, N_HEADS * dv)
    return {
        "cos": jnp.concatenate([cos, cos], axis=-1),
        "sin": jnp.concatenate([-sin, sin], axis=-1),
        "decq": jnp.exp(dec),
        "deck": jnp.exp(-dec) * dk ** -0.5,
        "gpow": gpow,
    }


def kernel(x_prompt, x_sample, state_gla, state_ret, meta, ln_g, ln_b, w_ffn1_up, w_ffn1_down, w_in,
           w_alpha2, b_alpha, b_merge, gn_gla, gn_ret, w_o_gla, w_o_ret, w_out, w_ffn2_up, w_ffn2_down):
    nb, seq, d = x_prompt.shape
    nbs, seq_s, _ = x_sample.shape
    depth = ln_g.shape[0]
    n_meta = meta.shape[0]
    _, _, _, dk, dv = state_gla.shape
    hk, hv = N_HEADS * dk, N_HEADS * dv
    rank = w_alpha2.shape[1]
    alpha = (2.0 * depth) ** 0.25
    blk = min(MAIN_BLOCK, seq)
    sblk = SMALL_BLOCK
    nseq = MAIN_SEQS if nb % MAIN_SEQS == 0 else 1
    assert n_meta <= sblk and seq_s <= sblk

    hs = jnp.zeros((1 + nbs, sblk, d), F32)
    hs = hs.at[0, :n_meta].set(meta.astype(F32)).at[1:, :seq_s].set(x_sample)
    n_valid_s = jnp.asarray([n_meta] + [seq_s] * nbs, jnp.int32)
    valid_s = (jnp.arange(sblk)[None, :, None] < n_valid_s[:, None, None]).astype(F32)
    valid_s = jnp.broadcast_to(valid_s, (1 + nbs, sblk, dk))
    off = jnp.asarray([0] + [n_meta + PAST_LEN] * nbs, jnp.int32)
    tabs_s = _mix_tables(off[:, None] + jnp.arange(sblk)[None, :], sblk, n_valid_s, dk, dv)
    tabs_p = _mix_tables(n_meta + jnp.arange(seq)[None, :], blk, jnp.asarray([blk], jnp.int32), dk, dv)
    hp = x_prompt

    c_rg = 2 * hk + hv
    c_ag = c_rg + hv
    c_qr = c_ag + rank
    c_gr = c_qr + 2 * hk + hv
    c_mg = c_gr + hv
    w = {
        "mix": jnp.concatenate([w_in[:, :, :c_rg], w_in[:, :, c_qr:c_gr]], axis=2).astype(BF16),
        "ag": jnp.pad(w_in[:, :, c_ag:c_qr], ((0, 0), (0, 0), (0, GATE_RANK_PAD - rank))).astype(BF16),
        "a2": jnp.pad(w_alpha2, ((0, 0), (0, GATE_RANK_PAD - rank), (0, 0))).astype(BF16),
        "g": jnp.concatenate([w_in[:, :, c_rg:c_ag], w_in[:, :, c_gr:c_mg], w_in[:, :, c_mg:]],
                             axis=2).astype(BF16),
        "oa": w_o_gla.astype(BF16),
        "ob": w_o_ret.astype(BF16),
        "o": w_out.astype(BF16),
    }
    w1u, w1d = w_ffn1_up.astype(BF16), w_ffn1_down.astype(BF16)
    w2u, w2d = w_ffn2_up.astype(BF16), w_ffn2_down.astype(BF16)

    gla_p, ret_p, gla_s, ret_s = [], [], [], []
    for l in range(depth):
        vec = {
            "ba": b_alpha[l][None, :],
            "gna": gn_gla[l][None, :],
            "gnr": gn_ret[l][None, :],
            "bm": b_merge[l][None, :],
            "lng": ln_g[l, 1][None, :],
            "lnb": ln_b[l, 1][None, :],
        }

        def layer(h3, gla0, ret0, tabs, bsz, ns, valid):
            shape = h3.shape
            h = _ffn_ln(h3.reshape(-1, d), w1u, w1d, l, ln_g[l, 0][None, :], ln_b[l, 0][None, :], alpha)
            h, sg, sr = _attn(h.reshape(shape), gla0, ret0, w, l, vec, tabs, bsz, ns, alpha, valid)
            h = _ffn_ln(h.reshape(-1, d), w2u, w2d, l, ln_g[l, 2][None, :], ln_b[l, 2][None, :], alpha)
            return h.reshape(shape), sg, sr

        zero = jnp.zeros((1, N_HEADS, dk, dv), F32)
        hs, sg_s, sr_s = layer(hs, jnp.concatenate([zero, state_gla[l].astype(F32)], axis=0),
                               jnp.concatenate([zero, state_ret[l].astype(F32)], axis=0),
                               tabs_s, sblk, 1, valid_s)
        hp, sg_p, sr_p = layer(hp, sg_s[:1], sr_s[:1], tabs_p, blk, nseq, None)
        gla_p.append(sg_p.astype(x_prompt.dtype))
        ret_p.append(sr_p.astype(x_prompt.dtype))
        gla_s.append(sg_s[1:].astype(state_gla.dtype))
        ret_s.append(sr_s[1:].astype(state_ret.dtype))

    return (hp, hs[1:, :seq_s], jnp.stack(gla_p, axis=0), jnp.stack(ret_p, axis=0),
            jnp.stack(gla_s, axis=0), jnp.stack(ret_s, axis=0))
```

```python
import functools

import numpy as np
import jax
import jax.numpy as jnp
from jax import lax
from jax.experimental import pallas as pl
from jax.experimental.pallas import tpu as pltpu

F32 = jnp.float32
BF16 = jnp.bfloat16

N_HEADS = 4
GATE_RANK_PAD = 128
GATE_TAU = 16.0
PAST_LEN = 1024
ROPE_BASE = 10000.0
LN_EPS = 1e-5
GN_EPS = 1e-5
LOG2E = 1.4426950408889634

SUBLANES = 8
MXU_WIDTH = 256
MAIN_BLOCK = 256
MAIN_SEQS = 1
MAIN_SUBBLOCKS = 2
STAGGER = 22
DIAG_BLOCK = 128
GATE_SLOTS = (1, 1, (1, 0, 1, 0, 1, 0, 1, 0), 0, (1, 1, 0, 0), 0)
SMALL_BLOCK = 32
ROW_TILE = 1024
ROW_SUB = 256
FFN_CHUNK = 6 * MXU_WIDTH
VMEM_LIMIT = 56 * 1024 * 1024

_NT = (((1,), (1,)), ((), ()))
_TN = (((0,), (0,)), ((), ()))


def _dot(a, b):
    return jnp.dot(a, b, preferred_element_type=F32)


def _dot_nt(a, b):
    return lax.dot_general(a, b, _NT, preferred_element_type=F32)


def _dot_tn(a, b):
    return lax.dot_general(a, b, _TN, preferred_element_type=F32)


def _layer_norm(x, g, b):
    mu = jnp.mean(x, axis=-1, keepdims=True)
    xc = x - mu
    var = jnp.mean(xc * xc, axis=-1, keepdims=True)
    return xc * lax.rsqrt(var + LN_EPS) * g + b


def _sigmoid(x):
    return 0.5 + 0.5 * jnp.tanh(0.5 * x)


def _silu(x):
    hx = 0.5 * x
    return hx + hx * jnp.tanh(hx)


def _log_sigmoid(x):
    return jnp.minimum(x, 0.0) - jnp.log1p(jnp.exp(-jnp.abs(x)))


def _resident(shape):
    return pl.BlockSpec(shape, lambda *_: (0,) * len(shape), pipeline_mode=pl.Buffered(1))


def _layer_of(stacked):
    rest = stacked.shape[1:]
    return pl.BlockSpec((None,) + rest, lambda *a: (a[-1][0],) + (0,) * len(rest),
                        pipeline_mode=pl.Buffered(1))


def _row_tile(n_rows, limit):
    t = min(limit, n_rows)
    while n_rows % t or t % SUBLANES:
        t -= SUBLANES
    return t


def _ffn_ln_kernel(_layer_ref, h_ref, wup_ref, wdn_ref, g_ref, b_ref, o_ref, *, d_ff, cuts, sub, alpha):
    n_sub = h_ref.shape[0] // sub
    n_ch = len(cuts) - 1
    items = [(s, j) for s in range(n_sub) for j in range(n_ch)]
    hb = [None] * n_sub
    acc = [None] * n_sub

    def up(s, j):
        if hb[s] is None:
            hb[s] = h_ref[s * sub:(s + 1) * sub, :].astype(BF16)
        return (_dot(hb[s], wup_ref[:, cuts[j]:cuts[j + 1]]),
                _dot(hb[s], wup_ref[:, d_ff + cuts[j]:d_ff + cuts[j + 1]]))

    def finish(s):
        rows = slice(s * sub, (s + 1) * sub)
        o_ref[rows, :] = _layer_norm(alpha * h_ref[rows, :] + 0.5 * acc[s], g_ref[...], b_ref[...])

    nxt = up(*items[0])
    ready = None
    for idx, (s, j) in enumerate(items):
        a, b = nxt
        if idx + 1 < len(items):
            nxt = up(*items[idx + 1])
        if ready is not None:
            finish(ready)
            ready = None
        y = _dot((_silu(a) * b).astype(BF16), wdn_ref[cuts[j]:cuts[j + 1], :])
        acc[s] = y if acc[s] is None else acc[s] + y
        if j == n_ch - 1:
            ready = s
    finish(ready)


def _ffn_ln(h, w_up, w_down, layer, g, b, alpha):
    n, d = h.shape
    d_ff = w_down.shape[1]
    cuts = tuple(range(0, d_ff, FFN_CHUNK)) + (d_ff,)
    tm = _row_tile(n, ROW_TILE)
    sub = ROW_SUB if tm % ROW_SUB == 0 else tm
    return pl.pallas_call(
        functools.partial(_ffn_ln_kernel, d_ff=d_ff, cuts=cuts, sub=sub, alpha=alpha),
        grid_spec=pltpu.PrefetchScalarGridSpec(
            num_scalar_prefetch=1,
            grid=(n // tm,),
            in_specs=[
                pl.BlockSpec((tm, d), lambda i, l: (i, 0)),
                _layer_of(w_up),
                _layer_of(w_down),
                _resident(g.shape),
                _resident(b.shape),
            ],
            out_specs=pl.BlockSpec((tm, d), lambda i, l: (i, 0)),
        ),
        out_shape=jax.ShapeDtypeStruct((n, d), F32),
        compiler_params=pltpu.CompilerParams(
            dimension_semantics=("parallel",), vmem_limit_bytes=VMEM_LIMIT),
        name="ffn_ln",
    )(jnp.asarray([layer], jnp.int32), h, w_up, w_down, g, b)


def _head_norm(o, gain):
    mu = jnp.mean(o, axis=-1, keepdims=True)
    oc = o - mu
    var = jnp.mean(oc * oc, axis=-1, keepdims=True)
    return oc * lax.rsqrt(var + GN_EPS) * gain


def _attn_kernel(_layer_ref, *refs, nseq, nsub, stagger, blk, dblk, dk, dv, masked, alpha):
    n_in = 24 if masked else 23
    (h_ref, wmix_ref, wag_ref, wa2_ref, ba_ref, cos_ref, sin_ref, decq_ref, deck_ref, tri_ref,
     lvl_ref, gla0_ref, ret0_ref, gna_ref, gnr_ref, gpow_ref, wg_ref, bm_ref, woa_ref, wob_ref,
     wo_ref, lng_ref, lnb_ref) = refs[:23]
    valid_ref = refs[23] if masked else None
    o_ref, sg_ref, sr_ref, sgt_scr, sr_scr, b_scr, oa_scr, ob_scr, q_scr, k_scr = refs[n_in:]
    hk = N_HEADS * dk
    hv = N_HEADS * dv
    d = h_ref.shape[-1]
    n_diag = blk // dblk
    lv_diag = dblk.bit_length() - 1
    n_lv = blk.bit_length() - 1
    c0 = 2 * hk + hv
    pw = d // 2
    n_gate = 4 * d // pw
    j = pl.program_id(1)

    def shared(ref, s):
        return min(s, ref.shape[0] - 1)

    @pl.when(j == 0)
    def _():
        for s in range(nseq):
            for hd in range(N_HEADS):
                sgt_scr[s, hd] = gla0_ref[shared(gla0_ref, s), hd].T
            sr_scr[s] = ret0_ref[shared(ret0_ref, s)]

    lvl = lvl_ref[...]
    r8 = lax.broadcasted_iota(jnp.int32, (blk, hk), 0) & (SUBLANES - 1)
    causal = (lax.broadcasted_iota(jnp.int32, (blk, blk), 0)
              >= lax.broadcasted_iota(jnp.int32, (blk, blk), 1))

    def tile(x, i, hd):
        return x[i * dblk:(i + 1) * dblk, hd * dk:(hd + 1) * dk]

    def block(s, u):
        rows = slice(u * blk, (u + 1) * blk)
        su = s * nsub + u
        hb = h_ref[s, rows, :].astype(BF16)
        gates = []

        def gate_pieces(n):
            for _ in range(n):
                p = len(gates)
                if p == n_gate:
                    return
                x = _dot(hb, wg_ref[:, p * pw:(p + 1) * pw])
                if p * pw < 2 * d:
                    gates.append(_silu(x))
                else:
                    gates.append(_sigmoid(x + bm_ref[:, p * pw - 2 * d:(p + 1) * pw - 2 * d]))

        def slot(k, i):
            n = GATE_SLOTS[k]
            return n if isinstance(n, int) else (n[i] if i < len(n) else 0)

        ag = _dot(hb, wag_ref[...])
        q_scr[su] = _dot(hb, wmix_ref[:, 0:hk]) * (dk ** -0.5)
        yield
        gate_in = _dot(ag.astype(BF16), wa2_ref[...])
        kg = _dot(hb, wmix_ref[:, hk:2 * hk])
        vg = _dot(hb, wmix_ref[:, 2 * hk:2 * hk + hv]).astype(BF16)
        gate_pieces(slot(0, 0))
        lg = _log_sigmoid(gate_in + ba_ref[...]) / GATE_TAU
        if masked:
            valid = valid_ref[s, rows, :]
            vmask = jnp.concatenate([valid] * N_HEADS, axis=1)
            lg = lg * vmask
            kg = kg * vmask
        k_scr[su] = kg
        yield
        qr = _dot(hb, wmix_ref[:, c0:c0 + hk])
        kr = _dot(hb, wmix_ref[:, c0 + hk:c0 + 2 * hk])
        lg_hi = lg.astype(BF16)
        rem = lg - lg_hi.astype(F32)
        lg_mid = rem.astype(BF16)
        lg_lo = (rem - lg_mid.astype(F32)).astype(BF16)
        tri = tri_ref[...]
        b2 = (_dot(tri, lg_hi) + _dot(tri, lg_mid) + _dot(tri, lg_lo)) * LOG2E
        b_scr[su] = b2
        vr = _dot(hb, wmix_ref[:, c0 + 2 * hk:c0 + 2 * hk + hv]).astype(BF16)
        gate_pieces(slot(1, 0))
        yield

        def in_group_rows(r):
            return jnp.concatenate(
                [jnp.broadcast_to(b_scr[su, pl.ds(SUBLANES * i + r, 1), :], (SUBLANES, hk))
                 for i in range(blk // SUBLANES)], axis=0)

        def level_factor(lv):
            if lv == 0:
                a = jnp.where(r8 < 2, in_group_rows(0),
                              jnp.where(r8 < 4, in_group_rows(2),
                                        jnp.where(r8 < 6, in_group_rows(4), in_group_rows(6))))
                diff = b2 - a
            elif lv == 1:
                diff = b2 - jnp.where(r8 < 4, in_group_rows(1), in_group_rows(5))
            elif lv == 2:
                diff = b2 - in_group_rows(3)
            else:
                grp = 2 << lv
                diff = jnp.concatenate(
                    [b2[g * grp:(g + 1) * grp] - b_scr[su, pl.ds(g * grp + grp // 2 - 1, 1), :]
                     for g in range(blk // grp)], axis=0)
            return jnp.exp2(-jnp.abs(diff))

        fac = []
        for lv in range(n_lv):
            gate_pieces(slot(2, lv))
            fac.append(level_factor(lv))
            yield
        last = b_scr[su, pl.ds(blk - 1, 1), :]
        q_in = (q_scr[su] * jnp.exp2(b2)).astype(BF16)
        k_out = (k_scr[su] * jnp.exp2(-jnp.abs(last - b2))).astype(BF16)
        s_dec = jnp.exp2(last)
        yield

        qgb = q_scr[su].astype(BF16)
        kgb = k_scr[su].astype(BF16)
        on_diag = lvl == lv_diag
        att = [[jnp.where(on_diag, _dot_nt(tile(qgb, i, hd), tile(kgb, i, hd)), 0.0)
                for hd in range(N_HEADS)] for i in range(n_diag)]
        yield
        for lv in range(lv_diag):
            qf = (q_scr[su] * fac[lv]).astype(BF16)
            kf = (k_scr[su] * fac[lv]).astype(BF16)
            at_level = lvl == lv
            for i in range(n_diag):
                for hd in range(N_HEADS):
                    att[i][hd] = jnp.where(at_level, _dot_nt(tile(qf, i, hd), tile(kf, i, hd)), att[i][hd])
            gate_pieces(slot(3, lv))
            yield
        if n_diag == 2:
            qf = (q_scr[su, dblk:, :] * fac[lv_diag][dblk:]).astype(BF16)
            kf = (k_scr[su, :dblk, :] * fac[lv_diag][:dblk]).astype(BF16)
        gna = gna_ref[...]
        for hd in range(N_HEADS):
            ks = slice(hd * dk, (hd + 1) * dk)
            vs = slice(hd * dv, (hd + 1) * dv)
            v = vg[:, vs]
            if n_diag == 2:
                below = _dot_nt(qf[:, ks], kf[:, ks])
                o = jnp.concatenate([
                    _dot(att[0][hd].astype(BF16), v[:dblk]),
                    _dot(jnp.concatenate([below, att[1][hd]], axis=1).astype(BF16), v)], axis=0)
            else:
                o = _dot(att[0][hd].astype(BF16), v)
            st = sgt_scr[s, hd]
            o = o + _dot_nt(q_in[:, ks], st.astype(BF16))
            sgt_scr[s, hd] = st * s_dec[:, ks] + _dot_tn(v, k_out[:, ks])
            oa_scr[su, :, vs] = _head_norm(o, gna[:, vs])
            gate_pieces(slot(4, hd))
            yield

        cos = cos_ref[shared(cos_ref, s), rows, :]
        sin = sin_ref[shared(sin_ref, s), rows, :]
        decq = decq_ref[...]
        deck = deck_ref[...]
        gpow = gpow_ref[shared(gpow_ref, s)]
        gnr = gnr_ref[...]
        for hd in range(N_HEADS):
            ks = slice(hd * dk, (hd + 1) * dk)
            vs = slice(hd * dv, (hd + 1) * dv)
            q = qr[:, ks]
            k = kr[:, ks]
            q = (q * cos + pltpu.roll(q, dk // 2, 1) * sin) * decq[:, ks]
            k = (k * cos + pltpu.roll(k, dk // 2, 1) * sin) * deck[:, ks]
            if masked:
                k = k * valid
            qb = q.astype(BF16)
            kb = k.astype(BF16)
            v = vr[:, vs]
            sc = jnp.where(causal, _dot_nt(qb, kb), 0.0)
            s0 = sr_scr[s, hd]
            o = _dot(sc.astype(BF16), v) + _dot(qb, s0.astype(BF16))
            sr_scr[s, hd] = (s0 + _dot_tn(kb, v)) * gpow[:, vs]
            ob_scr[su, :, vs] = _head_norm(o, gnr[:, vs])
            gate_pieces(slot(5, hd))
            yield

        gate_pieces(n_gate)
        rg, gr, ga, gb = [jnp.concatenate(gates[i * (d // pw):(i + 1) * (d // pw)], axis=1)
                          for i in range(4)]
        ya = _dot((rg * oa_scr[su]).astype(BF16), woa_ref[...])
        yb = _dot((gr * ob_scr[su]).astype(BF16), wob_ref[...])
        y = _dot((ga * ya + gb * yb).astype(BF16), wo_ref[...])
        o_ref[s, rows, :] = _layer_norm(alpha * h_ref[s, rows, :] + y, lng_ref[...], lnb_ref[...])
        yield

    waiting = [block(s, u) for u in range(nsub) for s in range(nseq)]
    running = []
    tick = 0
    while waiting or running:
        if waiting and tick % stagger == 0:
            running.append(waiting.pop(0))
        for g in list(running):
            try:
                next(g)
            except StopIteration:
                running.remove(g)
        tick += 1

    @pl.when(j == pl.num_programs(1) - 1)
    def _():
        for s in range(nseq):
            for hd in range(N_HEADS):
                sg_ref[s, hd] = sgt_scr[s, hd].T
            sr_ref[s] = sr_scr[s]


def _level_matrix(n):
    r = np.arange(n)[:, None]
    c = np.arange(n)[None, :]
    x = np.maximum(r ^ c, 1)
    lvl = np.where(r > c, np.floor(np.log2(x)), np.where(r == c, n.bit_length() - 1, -1))
    return lvl.astype(np.int32)


def _attn(h, gla0, ret0, w, layer, vec, tabs, blk, nseq, alpha, valid=None):
    nb, t, d = h.shape
    _, _, dk, dv = gla0.shape
    hk, hv = N_HEADS * dk, N_HEADS * dv
    dblk = min(blk, DIAG_BLOCK)
    nsub = MAIN_SUBBLOCKS if (valid is None and t % (MAIN_SUBBLOCKS * blk) == 0) else 1
    tstep = nsub * blk
    assert t % tstep == 0 and blk & (blk - 1) == 0 and SUBLANES <= blk <= 256 and blk // dblk <= 2
    assert nb % nseq == 0
    lvl = jnp.asarray(_level_matrix(dblk))
    tri = jnp.asarray(np.tril(np.ones((blk, blk), np.float32)), BF16)
    masked = valid is not None

    def per_seq(a, width):
        if a.shape[0] > 1:
            return pl.BlockSpec((nseq, tstep, width), lambda b, j, l: (b, j, 0))
        return pl.BlockSpec((1, tstep, width), lambda b, j, l: (0, j, 0))

    def state_in(a):
        if a.shape[0] > 1:
            return pl.BlockSpec((nseq, N_HEADS, dk, dv), lambda b, j, l: (b, 0, 0, 0))
        return pl.BlockSpec((1, N_HEADS, dk, dv), lambda b, j, l: (0, 0, 0, 0))

    seq = pl.BlockSpec((nseq, tstep, d), lambda b, j, l: (b, j, 0))
    args = [h, w["mix"], w["ag"], w["a2"], vec["ba"], tabs["cos"], tabs["sin"], tabs["decq"], tabs["deck"],
            tri, lvl, gla0, ret0, vec["gna"], vec["gnr"], tabs["gpow"], w["g"], vec["bm"], w["oa"], w["ob"],
            w["o"], vec["lng"], vec["lnb"]]
    in_specs = [
        seq,
        _layer_of(w["mix"]), _layer_of(w["ag"]), _layer_of(w["a2"]),
        _resident(vec["ba"].shape),
        per_seq(tabs["cos"], dk), per_seq(tabs["sin"], dk),
        _resident(tabs["decq"].shape), _resident(tabs["deck"].shape),
        _resident(tri.shape), _resident(lvl.shape),
        state_in(gla0), state_in(ret0),
        _resident(vec["gna"].shape), _resident(vec["gnr"].shape),
        (pl.BlockSpec((nseq, 1, hv), lambda b, j, l: (b, 0, 0)) if tabs["gpow"].shape[0] > 1
         else pl.BlockSpec((1, 1, hv), lambda b, j, l: (0, 0, 0))),
        _layer_of(w["g"]), _resident(vec["bm"].shape), _layer_of(w["oa"]),
        _layer_of(w["ob"]), _layer_of(w["o"]), _resident(vec["lng"].shape),
        _resident(vec["lnb"].shape),
    ]
    if masked:
        in_specs.append(pl.BlockSpec((nseq, tstep, dk), lambda b, j, l: (b, j, 0)))
        args.append(valid)
    state_out = pl.BlockSpec((nseq, N_HEADS, dk, dv), lambda b, j, l: (b, 0, 0, 0))
    return pl.pallas_call(
        functools.partial(_attn_kernel, nseq=nseq, nsub=nsub, stagger=STAGGER, blk=blk, dblk=dblk, dk=dk, dv=dv,
                          masked=masked, alpha=alpha),
        grid_spec=pltpu.PrefetchScalarGridSpec(
            num_scalar_prefetch=1,
            grid=(nb // nseq, t // tstep),
            in_specs=in_specs,
            out_specs=[seq, state_out, state_out],
            scratch_shapes=[
                pltpu.VMEM((nseq, N_HEADS, dv, dk), F32), pltpu.VMEM((nseq, N_HEADS, dk, dv), F32),
                pltpu.VMEM((nseq * nsub, blk, hk), F32), pltpu.VMEM((nseq * nsub, blk, hv), F32),
                pltpu.VMEM((nseq * nsub, blk, hv), F32), pltpu.VMEM((nseq * nsub, blk, hk), F32),
                pltpu.VMEM((nseq * nsub, blk, hk), F32)],
        ),
        out_shape=[jax.ShapeDtypeStruct((nb, t, d), F32),
                   jax.ShapeDtypeStruct((nb, N_HEADS, dk, dv), F32),
                   jax.ShapeDtypeStruct((nb, N_HEADS, dk, dv), F32)],
        compiler_params=pltpu.CompilerParams(
            dimension_semantics=("parallel", "arbitrary"), vmem_limit_bytes=VMEM_LIMIT),
        name="attn",
    )(jnp.asarray([layer], jnp.int32), *args)


def _mix_tables(pos, blk, n_valid, dk, dv):
    half = dk // 2
    inv = ROPE_BASE ** (-jnp.arange(half, dtype=F32) / half)
    ang = pos.astype(F32)[..., None] * inv
    cos = jnp.cos(ang)
    sin = jnp.sin(ang)
    log_gamma = jnp.log1p(-(2.0 ** (-5.0 - jnp.arange(N_HEADS, dtype=F32))))
    step = jnp.arange(blk, dtype=F32)[:, None, None] + 1.0
    dec = jnp.broadcast_to(step * log_gamma[None, :, None], (blk, N_HEADS, dk)).reshape(blk, N_HEADS * dk)
    gpow = jnp.exp(n_valid.astype(F32)[:, None, None] * log_gamma[None, :, None])
    gpow = jnp.broadcast_to(gpow, (pos.shape[0], N_HEADS, dv)).reshape(pos.shape[0], 1, N_HEADS * dv)
    return {
        "cos": jnp.concatenate([cos, cos], axis=-1),
        "sin": jnp.concatenate([-sin, sin], axis=-1),
        "decq": jnp.exp(dec),
        "deck": jnp.exp(-dec) * dk ** -0.5,
        "gpow": gpow,
    }


def kernel(x_prompt, x_sample, state_gla, state_ret, meta, ln_g, ln_b, w_ffn1_up, w_ffn1_down, w_in,
           w_alpha2, b_alpha, b_merge, gn_gla, gn_ret, w_o_gla, w_o_ret, w_out, w_ffn2_up, w_ffn2_down):
    nb, seq, d = x_prompt.shape
    nbs, seq_s, _ = x_sample.shape
    depth = ln_g.shape[0]
    n_meta = meta.shape[0]
    _, _, _, dk, dv = state_gla.shape
    hk, hv = N_HEADS * dk, N_HEADS * dv
    rank = w_alpha2.shape[1]
    alpha = (2.0 * depth) ** 0.25
    blk = min(MAIN_BLOCK, seq)
    sblk = SMALL_BLOCK
    nseq = MAIN_SEQS if nb % MAIN_SEQS == 0 else 1
    assert n_meta <= sblk and seq_s <= sblk

    hs = jnp.zeros((1 + nbs, sblk, d), F32)
    hs = hs.at[0, :n_meta].set(meta.astype(F32)).at[1:, :seq_s].set(x_sample)
    n_valid_s = jnp.asarray([n_meta] + [seq_s] * nbs, jnp.int32)
    valid_s = (jnp.arange(sblk)[None, :, None] < n_valid_s[:, None, None]).astype(F32)
    valid_s = jnp.broadcast_to(valid_s, (1 + nbs, sblk, dk))
    off = jnp.asarray([0] + [n_meta + PAST_LEN] * nbs, jnp.int32)
    tabs_s = _mix_tables(off[:, None] + jnp.arange(sblk)[None, :], sblk, n_valid_s, dk, dv)
    tabs_p = _mix_tables(n_meta + jnp.arange(seq)[None, :], blk, jnp.asarray([blk], jnp.int32), dk, dv)
    hp = x_prompt

    c_rg = 2 * hk + hv
    c_ag = c_rg + hv
    c_qr = c_ag + rank
    c_gr = c_qr + 2 * hk + hv
    c_mg = c_gr + hv
    w = {
        "mix": jnp.concatenate([w_in[:, :, :c_rg], w_in[:, :, c_qr:c_gr]], axis=2).astype(BF16),
        "ag": jnp.pad(w_in[:, :, c_ag:c_qr], ((0, 0), (0, 0), (0, GATE_RANK_PAD - rank))).astype(BF16),
        "a2": jnp.pad(w_alpha2, ((0, 0), (0, GATE_RANK_PAD - rank), (0, 0))).astype(BF16),
        "g": jnp.concatenate([w_in[:, :, c_rg:c_ag], w_in[:, :, c_gr:c_mg], w_in[:, :, c_mg:]],
                             axis=2).astype(BF16),
        "oa": w_o_gla.astype(BF16),
        "ob": w_o_ret.astype(BF16),
        "o": w_out.astype(BF16),
    }
    w1u, w1d = w_ffn1_up.astype(BF16), w_ffn1_down.astype(BF16)
    w2u, w2d = w_ffn2_up.astype(BF16), w_ffn2_down.astype(BF16)

    gla_p, ret_p, gla_s, ret_s = [], [], [], []
    for l in range(depth):
        vec = {
            "ba": b_alpha[l][None, :],
            "gna": gn_gla[l][None, :],
            "gnr": gn_ret[l][None, :],
            "bm": b_merge[l][None, :],
            "lng": ln_g[l, 1][None, :],
            "lnb": ln_b[l, 1][None, :],
        }

        def layer(h3, gla0, ret0, tabs, bsz, ns, valid):
            shape = h3.shape
            h = _ffn_ln(h3.reshape(-1, d), w1u, w1d, l, ln_g[l, 0][None, :], ln_b[l, 0][None, :], alpha)
            h, sg, sr = _attn(h.reshape(shape), gla0, ret0, w, l, vec, tabs, bsz, ns, alpha, valid)
            h = _ffn_ln(h.reshape(-1, d), w2u, w2d, l, ln_g[l, 2][None, :], ln_b[l, 2][None, :], alpha)
            return h.reshape(shape), sg, sr

        zero = jnp.zeros((1, N_HEADS, dk, dv), F32)
        hs, sg_s, sr_s = layer(hs, jnp.concatenate([zero, state_gla[l].astype(F32)], axis=0),
                               jnp.concatenate([zero, state_ret[l].astype(F32)], axis=0),
                               tabs_s, sblk, 1, valid_s)
        hp, sg_p, sr_p = layer(hp, sg_s[:1], sr_s[:1], tabs_p, blk, nseq, None)
        gla_p.append(sg_p.astype(x_prompt.dtype))
        ret_p.append(sr_p.astype(x_prompt.dtype))
        gla_s.append(sg_s[1:].astype(state_gla.dtype))
        ret_s.append(sr_s[1:].astype(state_ret.dtype))

    return (hp, hs[1:, :seq_s], jnp.stack(gla_p, axis=0), jnp.stack(ret_p, axis=0),
            jnp.stack(gla_s, axis=0), jnp.stack(ret_s, axis=0))
```
